```python
import math
import jax, jax.numpy as jnp
from jax import lax
import numpy as np


D_MODEL = 2048
BATCH = 2
SEQ = 4096
DEPTH = 1

GRID_W = 64
CTX_LEN = 256
DA_HEADS = 8
DA_DH = 64
DA_DV = 2 * DA_DH
GLA_HEADS = 4
GLA_DK = 128
GLA_DV = 256
GLA_RANK = 16
GLA_TAU = 16.0
GLA_CHUNK = 64
PEER_HEADS = 8
PEER_NKEYS = 128
PEER_NEXP = PEER_NKEYS * PEER_NKEYS
PEER_DQ = 256
PEER_TOPK = 16
PEER_TOK_BLOCK = 128
Q_BLOCK = 128
ROPE_BASE = 10000.0
EPS = 1e-6
DA_QK_W = DA_HEADS * 2 * DA_DH
DA_V_W = DA_HEADS * DA_DV
GLA_QK_W = GLA_HEADS * GLA_DK
GLA_V_W = GLA_HEADS * GLA_DV
IN_WIDTHS = (DA_QK_W, DA_QK_W, DA_V_W, GLA_QK_W, GLA_QK_W, GLA_V_W, GLA_V_W, 2 * GLA_RANK, D_MODEL, D_MODEL)
IN_W = sum(IN_WIDTHS)

kernel_name = 'hybrid_diffattn_gla_peer_dit_block'


def rmsnorm(x, g):
    xf = x.astype(jnp.float32)
    y = xf * lax.rsqrt(jnp.mean(xf * xf, axis=-1, keepdims=True) + EPS)
    return (y * g.astype(jnp.float32)).astype(x.dtype)


def modulate(xn, shift, scale):
    return xn * (1 + scale) + shift


def split_cols(t):
    idx = [int(i) for i in np.cumsum(IN_WIDTHS)[:-1]]
    return jnp.split(t, idx, axis=-1)


def axial_rope_tables(rows):
    row = jnp.repeat(jnp.arange(rows), GRID_W).astype(jnp.float32)
    col = jnp.tile(jnp.arange(GRID_W), rows).astype(jnp.float32)
    half = DA_DH // 2
    inv = ROPE_BASE ** (-jnp.arange(0, half, 2, dtype=jnp.float32) / half)
    ar = row[:, None] * inv
    ac = col[:, None] * inv
    return (jnp.cos(ar), jnp.sin(ar), jnp.cos(ac), jnp.sin(ac))


def rotate(x, cos, sin):
    cos = cos[None, :, None, None, :]
    sin = sin[None, :, None, None, :]
    x1, x2 = jnp.split(x, 2, axis=-1)
    return jnp.concatenate([x1 * cos - x2 * sin, x2 * cos + x1 * sin], axis=-1)


def apply_axial_rope(x, rope):
    cr, sr, cc, sc = rope
    xr, xc = jnp.split(x, 2, axis=-1)
    return jnp.concatenate([rotate(xr, cr, sr), rotate(xc, cc, sc)], axis=-1).astype(x.dtype)


def diff_attention(q, k, v, lam, lam_init, subln_g):
    B, Tq = q.shape[:2]
    nb = Tq // Q_BLOCK
    qb = jnp.moveaxis(q.reshape(B, nb, Q_BLOCK, DA_HEADS, 2, DA_DH), 1, 0)
    scale = DA_DH ** -0.5

    def one_block(qblk):
        s = jnp.einsum('bqhmd,bkhmd->bhmqk', qblk, k, preferred_element_type=jnp.float32) * scale
        p = jax.nn.softmax(s, axis=-1)
        a = p[:, :, 0] - lam * p[:, :, 1]
        return jnp.einsum('bhqk,bkhv->bqhv', a.astype(v.dtype), v)

    o = jnp.moveaxis(lax.map(one_block, qb), 0, 1).reshape(B, Tq, DA_HEADS, DA_DV)
    o = rmsnorm(o, subln_g) * (1.0 - lam_init)
    return o.reshape(B, Tq, DA_V_W)


def gla_chunked(q, k, v, log_a, s0):
    B, T, H, DK = q.shape
    DV = v.shape[-1]
    n = T // GLA_CHUNK
    r = lambda t: t.reshape(B, n, GLA_CHUNK, H, t.shape[-1]).astype(jnp.float32)
    qf, kf, vf, la = r(q), r(k), r(v), r(log_a)
    b = jnp.cumsum(la, axis=2)
    b_last = b[:, :, -1:]
    q_in = qf * jnp.exp(b)
    k_in = kf * jnp.exp(-b)
    k_st = kf * jnp.exp(b_last - b)
    mask = jnp.tril(jnp.ones((GLA_CHUNK, GLA_CHUNK), dtype=bool))
    att = jnp.where(mask, jnp.einsum('bnihd,bnjhd->bnhij', q_in, k_in), 0.0)
    o_intra = jnp.einsum('bnhij,bnjhv->bnihv', att, vf)
    st_chunk = jnp.einsum('bnjhd,bnjhv->bnhdv', k_st, vf)
    dec = jnp.exp(b_last[:, :, 0])

    def step(S, xs):
        qc, dc, sc = xs
        o = jnp.einsum('bihd,bhdv->bihv', qc, S)
        S = dc[..., None] * S + sc
        return S, o

    xs = (jnp.moveaxis(q_in, 1, 0), jnp.moveaxis(dec, 1, 0), jnp.moveaxis(st_chunk, 1, 0))
    s_fin, o_inter = lax.scan(step, s0.astype(jnp.float32), xs)
    o = o_intra + jnp.moveaxis(o_inter, 0, 1)
    return o.reshape(B, T, H, DV).astype(v.dtype), s_fin


def gla_inputs(gq, gk, gv, glr, w2_f, b_f, w2_b, b_b):
    B, T = gq.shape[:2]
    hs = lambda t, d: t.reshape(B, T, GLA_HEADS, d)
    q = hs(gq, GLA_DK) * GLA_DK ** -0.5
    k = hs(gk, GLA_DK)
    v = hs(gv, GLA_DV)
    la_f = jax.nn.log_sigmoid((glr[..., :GLA_RANK] @ w2_f + b_f).astype(jnp.float32)) / GLA_TAU
    la_b = jax.nn.log_sigmoid((glr[..., GLA_RANK:] @ w2_b + b_b).astype(jnp.float32)) / GLA_TAU
    return q, k, v, hs(la_f, GLA_DK), hs(la_b, GLA_DK)


def gla_output(o, gr, norm_g):
    B, T = o.shape[:2]
    return rmsnorm(o, norm_g).reshape(B, T, GLA_V_W) * jax.nn.silu(gr)


def merge_branches(y_a, y_b, ga, gb, w_br_a, w_br_b, w_out):
    return (jax.nn.sigmoid(ga) * (y_a @ w_br_a) + jax.nn.sigmoid(gb) * (y_b @ w_br_b)) @ w_out


def flip(t):
    return jnp.flip(t, axis=1)


def token_mixer(h, hc, rope, w_in, qn_g, kn_g, lam, lam_init, subln_g,
                w2_f, b_f, w2_b, b_b, gla_norm_g, w_br_a, w_br_b, w_out, with_ctx_out):
    B, T, _ = h.shape
    Tc = hc.shape[1]
    (dq, dk, dv, gq, gk, gv, gr, glr, ga, gb) = split_cols(h @ w_in)
    (cdq, cdk, cdv, cgq, cgk, cgv, cgr, cglr, cga, cgb) = split_cols(hc @ w_in)
    da_shape = lambda t, n: t.reshape(B, n, DA_HEADS, 2, DA_DH)

    q = apply_axial_rope(rmsnorm(da_shape(dq, T), qn_g), rope)
    k_c = rmsnorm(da_shape(cdk, Tc), kn_g)
    v_c = cdv.reshape(B, Tc, DA_HEADS, DA_DV)
    k_all = jnp.concatenate([k_c, apply_axial_rope(rmsnorm(da_shape(dk, T), kn_g), rope)], axis=1)
    v_all = jnp.concatenate([v_c, dv.reshape(B, T, DA_HEADS, DA_DV)], axis=1)
    y_a = diff_attention(q, k_all, v_all, lam, lam_init, subln_g)

    q_g, k_g, v_g, la_f, la_b = gla_inputs(gq, gk, gv, glr, w2_f, b_f, w2_b, b_b)
    cq_g, ck_g, cv_g, cla_f, cla_b = gla_inputs(cgq, cgk, cgv, cglr, w2_f, b_f, w2_b, b_b)
    zero = jnp.zeros((B, GLA_HEADS, GLA_DK, GLA_DV), jnp.float32)
    o_cf, s_f = gla_chunked(cq_g, ck_g, cv_g, cla_f, zero)
    o_cb, s_b = gla_chunked(flip(cq_g), flip(ck_g), flip(cv_g), flip(cla_b), zero)
    o_f, _ = gla_chunked(q_g, k_g, v_g, la_f, s_f)
    o_b, _ = gla_chunked(flip(q_g), flip(k_g), flip(v_g), flip(la_b), s_b)
    y_b = gla_output(o_f + flip(o_b), gr, gla_norm_g)

    y = merge_branches(y_a, y_b, ga, gb, w_br_a, w_br_b, w_out)
    if not with_ctx_out:
        return y, None
    q_c = rmsnorm(da_shape(cdq, Tc), qn_g)
    y_ca = diff_attention(q_c, k_c, v_c, lam, lam_init, subln_g)
    y_cb = gla_output(o_cf + flip(o_cb), cgr, gla_norm_g)
    y_c = merge_branches(y_ca, y_cb, cga, cgb, w_br_a, w_br_b, w_out)
    return y, y_c


def peer_ffn(h, wq, keys, u, v):
    B, T, D = h.shape
    q = (h @ wq).reshape(B, T, PEER_HEADS, 2, PEER_DQ // 2)
    s = jnp.einsum('bthpd,hpkd->bthpk', q, keys, preferred_element_type=jnp.float32)
    s1, i1 = lax.top_k(s[..., 0, :], PEER_TOPK)
    s2, i2 = lax.top_k(s[..., 1, :], PEER_TOPK)
    cand = (s1[..., :, None] + s2[..., None, :]).reshape(B, T, PEER_HEADS, PEER_TOPK * PEER_TOPK)
    cidx = (i1[..., :, None] * PEER_NKEYS + i2[..., None, :]).reshape(B, T, PEER_HEADS, PEER_TOPK * PEER_TOPK)
    top_s, pos = lax.top_k(cand, PEER_TOPK)
    idx = jnp.take_along_axis(cidx, pos, axis=-1)
    g = jax.nn.softmax(top_s, axis=-1)
    nblk = (B * T) // PEER_TOK_BLOCK
    hf = h.reshape(nblk, PEER_TOK_BLOCK, D)
    idxf = idx.reshape(nblk, PEER_TOK_BLOCK, PEER_HEADS * PEER_TOPK)
    gf = g.reshape(nblk, PEER_TOK_BLOCK, PEER_HEADS * PEER_TOPK).astype(h.dtype)

    def one_block(args):
        xb, ib, gb = args
        act = jax.nn.gelu(jnp.einsum('pd,ped->pe', xb, u[ib]), approximate=False)
        return jnp.einsum('pe,ped->pd', act * gb, v[ib])

    return lax.map(one_block, (hf, idxf, gf)).reshape(B, T, D)


def setup_inputs(seed: int = 0) -> dict:
    key = jax.random.key(seed)
    ks = list(jax.random.split(key, 32))
    nrm = lambda shape, s: jax.random.normal(ks.pop(), shape, jnp.float32) * s
    D = D_MODEL
    return {
        'x': nrm((BATCH, SEQ, D), 1.0),
        'c': nrm((BATCH, D), 1.0),
        'ctx': nrm((BATCH, CTX_LEN, D), 1.0),
        'c_ctx': nrm((D,), 1.0),
        'ada_w': nrm((DEPTH, D, 6 * D), 0.5 * D ** -0.5),
        'ada_b': nrm((DEPTH, 6 * D), 0.01),
        'norm1_g': 1.0 + nrm((DEPTH, D), 0.02),
        'norm2_g': 1.0 + nrm((DEPTH, D), 0.02),
        'w_in': nrm((DEPTH, D, IN_W), D ** -0.5),
        'da_qn_g': 1.0 + nrm((DEPTH, DA_DH), 0.02),
        'da_kn_g': 1.0 + nrm((DEPTH, DA_DH), 0.02),
        'da_lam_q1': nrm((DEPTH, DA_DH), 0.1),
        'da_lam_k1': nrm((DEPTH, DA_DH), 0.1),
        'da_lam_q2': nrm((DEPTH, DA_DH), 0.1),
        'da_lam_k2': nrm((DEPTH, DA_DH), 0.1),
        'da_subln_g': 1.0 + nrm((DEPTH, DA_DV), 0.02),
        'gla_w2_f': nrm((DEPTH, GLA_RANK, GLA_QK_W), GLA_RANK ** -0.5),
        'gla_b_f': nrm((DEPTH, GLA_QK_W), 0.1),
        'gla_w2_b': nrm((DEPTH, GLA_RANK, GLA_QK_W), GLA_RANK ** -0.5),
        'gla_b_b': nrm((DEPTH, GLA_QK_W), 0.1),
        'gla_norm_g': 1.0 + nrm((DEPTH, GLA_DV), 0.02),
        'w_br_a': nrm((DEPTH, DA_V_W, D), DA_V_W ** -0.5),
        'w_br_b': nrm((DEPTH, GLA_V_W, D), GLA_V_W ** -0.5),
        'w_out': nrm((DEPTH, D, D), D ** -0.5),
        'peer_wq': nrm((DEPTH, D, PEER_HEADS * PEER_DQ), D ** -0.5),
        'peer_keys': nrm((DEPTH, PEER_HEADS, 2, PEER_NKEYS, PEER_DQ // 2), (PEER_DQ // 2) ** -0.5),
        'peer_u': nrm((DEPTH, PEER_NEXP, D), D ** -0.5),
        'peer_v': nrm((DEPTH, PEER_NEXP, D), 1.0),
    }


def reference(x, c, ctx, c_ctx, ada_w, ada_b, norm1_g, norm2_g, w_in, da_qn_g, da_kn_g,
              da_lam_q1, da_lam_k1, da_lam_q2, da_lam_k2, da_subln_g, gla_w2_f, gla_b_f,
              gla_w2_b, gla_b_b, gla_norm_g, w_br_a, w_br_b, w_out, peer_wq, peer_keys,
              peer_u, peer_v):
    T = x.shape[1]
    rows = T // GRID_W
    rope = axial_rope_tables(rows)
    for l in range(DEPTH):
        has_next = l + 1 < DEPTH
        mod = jax.nn.silu(c) @ ada_w[l] + ada_b[l]
        mod_c = jax.nn.silu(c_ctx) @ ada_w[l] + ada_b[l]
        sh1, sc1, g1, sh2, sc2, g2 = jnp.split(mod[:, None, :], 6, axis=-1)
        csh1, csc1, cg1, csh2, csc2, cg2 = jnp.split(mod_c, 6, axis=-1)
        lam_init = 0.8 - 0.6 * math.exp(-0.3 * l)
        lam = (jnp.exp(jnp.sum((da_lam_q1[l] * da_lam_k1[l]).astype(jnp.float32)))
               - jnp.exp(jnp.sum((da_lam_q2[l] * da_lam_k2[l]).astype(jnp.float32))) + lam_init)
        h = modulate(rmsnorm(x, norm1_g[l]), sh1, sc1)
        hc = modulate(rmsnorm(ctx, norm1_g[l]), csh1, csc1)
        y, y_c = token_mixer(h, hc, rope, w_in[l], da_qn_g[l], da_kn_g[l], lam, lam_init,
                             da_subln_g[l], gla_w2_f[l], gla_b_f[l], gla_w2_b[l], gla_b_b[l],
                             gla_norm_g[l], w_br_a[l], w_br_b[l], w_out[l], has_next)
        x = x + g1 * y
        x = x + g2 * peer_ffn(modulate(rmsnorm(x, norm2_g[l]), sh2, sc2),
                              peer_wq[l], peer_keys[l], peer_u[l], peer_v[l])
        if has_next:
            ctx = ctx + cg1 * y_c
            ctx = ctx + cg2 * peer_ffn(modulate(rmsnorm(ctx, norm2_g[l]), csh2, csc2),
                                       peer_wq[l], peer_keys[l], peer_u[l], peer_v[l])
    return x
```

```python
import functools
import math

import jax
import jax.numpy as jnp
import numpy as np
from jax import lax
from jax.experimental import pallas as pl
from jax.experimental.pallas import tpu as pltpu

F32 = jnp.float32
BF16 = jnp.bfloat16

GRID_W = 64
DA_HEADS = 8
DA_DH = 64
DA_DV = 128
GLA_HEADS = 4
GLA_DK = 128
GLA_DV = 256
GLA_RANK = 16
GLA_TAU = 16.0
GLA_CHUNK = 64
PEER_HEADS = 8
PEER_NKEYS = 128
PEER_TOPK = 16
ROPE_BASE = 10000.0
EPS = 1e-6
LAM_INIT = 0.8 - 0.6 * math.exp(-0.3 * 0)

LANES = 128
ROW_TILE = 256
VMEM_LIMIT = 56 * 1024 * 1024

NEG_INF = float("-inf")


def _cparams(sem):
    return pltpu.CompilerParams(dimension_semantics=sem, vmem_limit_bytes=VMEM_LIMIT)


def _split2(x):
    hi = x.astype(BF16)
    lo = (x - hi.astype(F32)).astype(BF16)
    return hi, lo


def _split3(x):
    p1 = x.astype(BF16)
    r1 = x - p1.astype(F32)
    p2 = r1.astype(BF16)
    p3 = (r1 - p2.astype(F32)).astype(BF16)
    return p1, p2, p3


def _dot(a, b):
    return jnp.dot(a, b, preferred_element_type=F32)


def _dot_nt(a, b):
    return lax.dot_general(a, b, (((1,), (1,)), ((), ())), preferred_element_type=F32)


def _dot_tn(a, b):
    return lax.dot_general(a, b, (((0,), (0,)), ((), ())), preferred_element_type=F32)


def _dot3_nt(a, b):
    ah, al = _split2(a)
    bh, bl = _split2(b)
    return _dot_nt(ah, bh) + _dot_nt(ah, bl) + _dot_nt(al, bh)


def _dot3(a, b):
    ah, al = _split2(a)
    bh, bl = _split2(b)
    return _dot(ah, bh) + _dot(ah, bl) + _dot(al, bh)


def _exact_lhs_dot(m_bf16, x):
    p1, p2, p3 = _split3(x)
    return _dot(m_bf16, p1) + _dot(m_bf16, p2) + _dot(m_bf16, p3)


def _ada_kernel(cb_ref, w_ref, b_ref, o_ref, *, n_rows):
    tn = w_ref.shape[1]
    o_ref[...] = jnp.zeros(o_ref.shape, F32)
    for r in range(n_rows):
        cv = cb_ref[r]
        s = cv * (1.0 / (1.0 + jnp.exp(-cv)))
        for j in range(tn // LANES):
            sl = slice(j * LANES, (j + 1) * LANES)
            col = jnp.sum(s * w_ref[:, sl], axis=0, keepdims=True)
            o_ref[r:r + 1, sl] = col + b_ref[:, sl]


def _ada_modulation(cond, ada_w, ada_b):
    n_rows, d = cond.shape
    n = ada_w.shape[1]
    tn = 1024
    cb = jnp.broadcast_to(cond[:, :, None], (n_rows, d, LANES))
    return pl.pallas_call(
        functools.partial(_ada_kernel, n_rows=n_rows),
        out_shape=jax.ShapeDtypeStruct((8, n), F32),
        grid=(n // tn,),
        in_specs=[
            pl.BlockSpec((n_rows, d, LANES), lambda j: (0, 0, 0)),
            pl.BlockSpec((d, tn), lambda j: (0, j)),
            pl.BlockSpec((1, tn), lambda j: (0, j)),
        ],
        out_specs=pl.BlockSpec((8, tn), lambda j: (0, j)),
        compiler_params=_cparams(("arbitrary",)),
        name="ada_modulation",
    )(cb, ada_w, ada_b.reshape(1, n))


def _rms_mod(xf, g, shift, scale):
    var = jnp.mean(xf * xf, axis=-1, keepdims=True)
    y = xf * lax.rsqrt(var + EPS) * g
    return y * (1.0 + scale) + shift


def _norm_mod_kernel(x_ref, ctx_ref, ss_ref, g_ref, o_ref, *, nc):
    i = pl.program_id(1)

    def emit(src):
        h = _rms_mod(src[0], g_ref[...], ss_ref[0, 0:1, :], ss_ref[0, 1:2, :])
        o_ref[0] = h.astype(o_ref.dtype)

    @pl.when(i < nc)
    def _():
        emit(ctx_ref)

    @pl.when(i >= nc)
    def _():
        emit(x_ref)


def _norm_mod_all(x, ctx, ss, g):
    b, t, d = x.shape
    tc = ctx.shape[1]
    tm = ROW_TILE
    nc = tc // tm
    r = tc + t
    return pl.pallas_call(
        functools.partial(_norm_mod_kernel, nc=nc),
        out_shape=jax.ShapeDtypeStruct((b, r, d), BF16),
        grid=(b, r // tm),
        in_specs=[
            pl.BlockSpec((1, tm, d), lambda bi, i: (bi, jnp.maximum(i - nc, 0), 0)),
            pl.BlockSpec((1, tm, d), lambda bi, i: (bi, jnp.minimum(i, nc - 1), 0)),
            pl.BlockSpec((1, 2, d), lambda bi, i: (jnp.where(i < nc, b, bi), 0, 0)),
            pl.BlockSpec((1, d), lambda bi, i: (0, 0)),
        ],
        out_specs=pl.BlockSpec((1, tm, d), lambda bi, i: (bi, i, 0)),
        compiler_params=_cparams(("arbitrary", "arbitrary")),
        name="norm_mod",
    )(x, ctx, ss, g)


def _matmul_kernel(a_ref, b_ref, o_ref):
    o_ref[...] = _dot(a_ref[...], b_ref[...]).astype(o_ref.dtype)


def _matmul(a, b, tm, tn, out_dtype, name):
    m, k = a.shape
    n = b.shape[1]
    return pl.pallas_call(
        _matmul_kernel,
        out_shape=jax.ShapeDtypeStruct((m, n), out_dtype),
        grid=(n // tn, m // tm),
        in_specs=[
            pl.BlockSpec((tm, k), lambda j, i: (i, 0)),
            pl.BlockSpec((k, tn), lambda j, i: (0, j)),
        ],
        out_specs=pl.BlockSpec((tm, tn), lambda j, i: (i, j)),
        compiler_params=_cparams(("arbitrary", "arbitrary")),
        name=name,
    )(a, b)


def _rope_tables(t_lat, t_ctx):
    rows = t_lat // GRID_W
    row = np.repeat(np.arange(rows), GRID_W).astype(np.float32)
    col = np.tile(np.arange(GRID_W), rows).astype(np.float32)
    half = DA_DH // 2
    inv = jnp.asarray(ROPE_BASE, F32) ** (-jnp.arange(0, half, 2, dtype=F32) / half)
    ar = jnp.asarray(row)[:, None] * inv
    ac = jnp.asarray(col)[:, None] * inv
    cr, sr, cc, sc = jnp.cos(ar), jnp.sin(ar), jnp.cos(ac), jnp.sin(ac)
    z = jnp.zeros_like(sr)
    cos64 = jnp.concatenate([cr, cr, cc, cc], axis=1)
    sa64 = jnp.concatenate([-sr, z, -sc, z], axis=1)
    sb64 = jnp.concatenate([z, sr, z, sc], axis=1)
    tile2 = lambda a: jnp.concatenate([a, a], axis=1)
    pad = lambda a, v: jnp.concatenate([jnp.full((t_ctx, LANES), v, F32), tile2(a)], axis=0)
    return pad(cos64, 1.0), pad(sa64, 0.0), pad(sb64, 0.0)


def _qk_prep_kernel(x_ref, cos_ref, sa_ref, sb_ref, g_ref, bd_ref, o_ref, *, is_q):
    x = x_ref[0].astype(F32)
    ss = _exact_lhs_dot_rhs(x * x, bd_ref[...])
    y = x * lax.rsqrt(ss * (1.0 / DA_DH) + EPS) * g_ref[0]
    up = pltpu.roll(y, LANES - 16, axis=1)
    dn = pltpu.roll(y, 16, axis=1)
    o = y * cos_ref[...] + up * sa_ref[...] + dn * sb_ref[...]
    if is_q:
        o = o * (DA_DH ** -0.5)
        lane = lax.broadcasted_iota(jnp.int32, o.shape, 1)
        zero = jnp.zeros_like(o)
        o_ref[0, :, :LANES] = jnp.where(lane < DA_DH, o, zero).astype(o_ref.dtype)
        o_ref[0, :, LANES:] = jnp.where(lane >= DA_DH, o, zero).astype(o_ref.dtype)
    else:
        o_ref[0] = o.astype(o_ref.dtype)


def _exact_lhs_dot_rhs(x, m_bf16):
    p1, p2, p3 = _split3(x)
    return _dot(p1, m_bf16) + _dot(p2, m_bf16) + _dot(p3, m_bf16)


def _qk_prep(proj3, cos_t, sa_t, sb_t, gain, col0, is_q, t_ctx):
    b, r, _ = proj3.shape
    tm = ROW_TILE
    row0 = t_ctx // tm if is_q else 0
    n_rows = r - t_ctx if is_q else r
    width = 2 * LANES if is_q else LANES
    bd = jnp.asarray(np.kron(np.eye(2), np.ones((DA_DH, DA_DH))), BF16)
    g2 = jnp.concatenate([gain, gain]).reshape(1, 1, LANES).astype(F32)
    return pl.pallas_call(
        functools.partial(_qk_prep_kernel, is_q=is_q),
        out_shape=jax.ShapeDtypeStruct((b, n_rows, DA_HEADS * width), BF16),
        grid=(b, n_rows // tm, DA_HEADS),
        in_specs=[
            pl.BlockSpec((1, tm, LANES), lambda bi, i, h: (bi, row0 + i, col0 + h)),
            pl.BlockSpec((tm, LANES), lambda bi, i, h: (row0 + i, 0)),
            pl.BlockSpec((tm, LANES), lambda bi, i, h: (row0 + i, 0)),
            pl.BlockSpec((tm, LANES), lambda bi, i, h: (row0 + i, 0)),
            pl.BlockSpec((1, 1, LANES), lambda bi, i, h: (0, 0, 0)),
            pl.BlockSpec((LANES, LANES), lambda bi, i, h: (0, 0)),
        ],
        out_specs=pl.BlockSpec((1, tm, width), lambda bi, i, h: (bi, i, h)),
        compiler_params=_cparams(("arbitrary", "arbitrary", "arbitrary")),
        name="q_prep" if is_q else "k_prep",
    )(proj3, cos_t, sa_t, sb_t, g2, bd)


def _attn_kernel(q_ref, k_ref, v_ref, lam_ref, g_ref, o_ref, *, tk, nk):
    tq = q_ref.shape[1]
    q = q_ref[0]
    qs = jnp.concatenate([q[:, :LANES], q[:, LANES:]], axis=0)

    def step(c, carry):
        m, l, acc = carry
        off = pl.multiple_of(c * tk, tk)
        kc = k_ref[0, pl.ds(off, tk), :]
        vc = v_ref[0, pl.ds(off, tk), :]
        s = _dot_nt(qs, kc)
        m_new = jnp.maximum(m, jnp.max(s, axis=1, keepdims=True))
        alpha = jnp.exp(m - m_new)
        p = jnp.exp(s - m_new)
        l = alpha * l + jnp.sum(p, axis=1, keepdims=True)
        acc = alpha * acc + _dot(p.astype(BF16), vc)
        return m_new, l, acc

    init = (jnp.full((2 * tq, 1), NEG_INF, F32), jnp.zeros((2 * tq, 1), F32),
            jnp.zeros((2 * tq, DA_DV), F32))
    _, l, acc = lax.fori_loop(0, nk, step, init)
    lam = (jnp.exp(jnp.sum(lam_ref[0:1, :] * lam_ref[1:2, :], axis=1, keepdims=True))
           - jnp.exp(jnp.sum(lam_ref[2:3, :] * lam_ref[3:4, :], axis=1, keepdims=True)) + LAM_INIT)
    o = acc[:tq] / l[:tq] - lam * (acc[tq:] / l[tq:])
    var = jnp.mean(o * o, axis=-1, keepdims=True)
    o = o * lax.rsqrt(var + EPS) * g_ref[...] * (1.0 - LAM_INIT)
    o_ref[0] = o.astype(o_ref.dtype)


def _diff_attention(qm, kh, proj3, v_col0, lam_rows, subln_g):
    b, t, _ = qm.shape
    r = kh.shape[1]
    tq = ROW_TILE
    tk = ROW_TILE
    return pl.pallas_call(
        functools.partial(_attn_kernel, tk=tk, nk=r // tk),
        out_shape=jax.ShapeDtypeStruct((b, t, DA_HEADS * DA_DV), BF16),
        grid=(b, DA_HEADS, t // tq),
        in_specs=[
            pl.BlockSpec((1, tq, 2 * LANES), lambda bi, h, i: (bi, i, h)),
            pl.BlockSpec((1, r, LANES), lambda bi, h, i: (bi, 0, h)),
            pl.BlockSpec((1, r, LANES), lambda bi, h, i: (bi, 0, v_col0 + h)),
            pl.BlockSpec((8, LANES), lambda bi, h, i: (0, 0)),
            pl.BlockSpec((1, LANES), lambda bi, h, i: (0, 0)),
        ],
        out_specs=pl.BlockSpec((1, tq, DA_DV), lambda bi, h, i: (bi, i, h)),
        compiler_params=_cparams(("arbitrary", "arbitrary", "arbitrary")),
        name="diff_attention",
    )(qm, kh, proj3, lam_rows, subln_g.reshape(1, DA_DV).astype(F32))


def _log_sigmoid(z):
    return jnp.minimum(z, 0.0) - jnp.log(1.0 + jnp.exp(-jnp.abs(z)))


def _gla_kernel(*refs, reverse, final, n_chunks):
    if final:
        (q_ref, k_ref, v_ref, lr_ref, w2_ref, b_ref, tri_ref, ones_ref,
         of_ref, gr_ref, ng_ref, o_ref, st_ref) = refs
    else:
        (q_ref, k_ref, v_ref, lr_ref, w2_ref, b_ref, tri_ref, ones_ref, o_ref, st_ref) = refs
    c_len = GLA_CHUNK

    @pl.when(pl.program_id(2) == 0)
    def _():
        st_ref[...] = jnp.zeros(st_ref.shape, F32)

    q = q_ref[0].astype(F32) * (GLA_DK ** -0.5)
    k = k_ref[0].astype(F32)
    v = v_ref[0]
    z = _dot3(lr_ref[0], w2_ref[0]) + b_ref[0]
    la = _log_sigmoid(z) * (1.0 / GLA_TAU)
    tri = tri_ref[...]
    cum = _exact_lhs_dot(tri, la)
    tot = _exact_lhs_dot(ones_ref[...], la)
    q_in = (q * jnp.exp(cum)).astype(BF16)
    k_in = (k * jnp.exp(-cum)).astype(BF16)
    k_st = (k * jnp.exp(tot - cum)).astype(BF16)
    dec = jnp.exp(tot)
    att = jnp.where(tri > 0, _dot_nt(q_in, k_in), 0.0)
    o_intra = _dot(att.astype(BF16), v)

    order = range(n_chunks - 1, -1, -1) if reverse else range(n_chunks)
    outs = [None] * n_chunks
    for c in order:
        sl = slice(c * c_len, (c + 1) * c_len)
        st = st_ref[...]
        outs[c] = o_intra[sl] + _dot_nt(q_in[sl], st.astype(BF16))
        kv = _dot_tn(v[sl], k_st[sl])
        st_ref[...] = st * dec[c * c_len:c * c_len + 1, :] + kv
    o = jnp.concatenate(outs, axis=0)
    if final:
        o = o + of_ref[0].astype(F32)
        var = jnp.mean(o * o, axis=-1, keepdims=True)
        y = o * lax.rsqrt(var + EPS) * ng_ref[...]
        gr = gr_ref[0].astype(F32)
        y = y * (gr * (1.0 / (1.0 + jnp.exp(-gr))))
        o_ref[0] = y.astype(o_ref.dtype)
    else:
        o_ref[0] = o.astype(o_ref.dtype)


def _gla_direction(proj3, lr3, w2p, bias, t_ctx, cols, reverse, o_fwd=None, norm_g=None):
    b, r, _ = proj3.shape
    tc = ROW_TILE
    n_chunks = tc // GLA_CHUNK
    ng = r // tc
    ncg = t_ctx // tc
    final = o_fwd is not None
    q_col, k_col, v_col, gr_col = cols

    blk = np.kron(np.eye(n_chunks), np.ones((GLA_CHUNK, GLA_CHUNK)))
    full_tri = np.tril(np.ones((tc, tc)))
    tri = blk * (full_tri.T if reverse else full_tri)
    tri = jnp.asarray(tri, BF16)
    ones_bd = jnp.asarray(blk, BF16)

    if reverse:
        def grp(s):
            return jnp.where(s < ncg, ncg - 1 - s, ng - 1 - (s - ncg))
    else:
        def grp(s):
            return s

    def lat(s):
        return jnp.maximum(grp(jnp.maximum(s, ncg)) - ncg, 0)

    in_specs = [
        pl.BlockSpec((1, tc, GLA_DK), lambda bi, h, s: (bi, grp(s), q_col + h)),
        pl.BlockSpec((1, tc, GLA_DK), lambda bi, h, s: (bi, grp(s), k_col + h)),
        pl.BlockSpec((1, tc, GLA_DV), lambda bi, h, s: (bi, grp(s), v_col + h)),
        pl.BlockSpec((1, tc, LANES), lambda bi, h, s: (bi, grp(s), 0)),
        pl.BlockSpec((1, LANES, GLA_DK), lambda bi, h, s: (h, 0, 0)),
        pl.BlockSpec((1, 1, GLA_DK), lambda bi, h, s: (h, 0, 0)),
        pl.BlockSpec((tc, tc), lambda bi, h, s: (0, 0)),
        pl.BlockSpec((tc, tc), lambda bi, h, s: (0, 0)),
    ]
    args = [proj3, proj3, proj3, lr3, w2p, bias, tri, ones_bd]
    if final:
        in_specs += [
            pl.BlockSpec((1, tc, GLA_DV), lambda bi, h, s: (bi, lat(s), h)),
            pl.BlockSpec((1, tc, GLA_DV), lambda bi, h, s: (bi, grp(s), gr_col + h)),
            pl.BlockSpec((1, GLA_DV), lambda bi, h, s: (0, 0)),
        ]
        args += [o_fwd, proj3, norm_g.reshape(1, GLA_DV).astype(F32)]
    return pl.pallas_call(
        functools.partial(_gla_kernel, reverse=reverse, final=final, n_chunks=n_chunks),
        out_shape=jax.ShapeDtypeStruct((b, r - t_ctx, GLA_HEADS * GLA_DV), BF16 if final else F32),
        grid=(b, GLA_HEADS, ng),
        in_specs=in_specs,
        out_specs=pl.BlockSpec((1, tc, GLA_DV), lambda bi, h, s: (bi, lat(s), h)),
        scratch_shapes=[pltpu.VMEM((GLA_DV, GLA_DK), F32)],
        compiler_params=_cparams(("arbitrary", "arbitrary", "arbitrary")),
        name="gla_bwd" if reverse else "gla_fwd",
    )(*args)


def _sigmoid(x):
    return 1.0 / (1.0 + jnp.exp(-x))


def _merge_kernel(ya_ref, yb_ref, wa_ref, wb_ref, ga_ref, gb_ref, o_ref):
    a = _dot(ya_ref[0], wa_ref[...])
    bb = _dot(yb_ref[0], wb_ref[...])
    z = _sigmoid(ga_ref[0].astype(F32)) * a + _sigmoid(gb_ref[0].astype(F32)) * bb
    o_ref[0] = z.astype(o_ref.dtype)


def _merge(y_a, y_b, wa, wb, proj3, ga_col0, gb_col0, t_ctx):
    b, t, ka = y_a.shape
    kb = y_b.shape[2]
    d = wa.shape[1]
    tm, tn = ROW_TILE, 1024
    row0 = t_ctx // tm
    return pl.pallas_call(
        _merge_kernel,
        out_shape=jax.ShapeDtypeStruct((b, t, d), BF16),
        grid=(d // tn, b, t // tm),
        in_specs=[
            pl.BlockSpec((1, tm, ka), lambda j, bi, i: (bi, i, 0)),
            pl.BlockSpec((1, tm, kb), lambda j, bi, i: (bi, i, 0)),
            pl.BlockSpec((ka, tn), lambda j, bi, i: (0, j)),
            pl.BlockSpec((kb, tn), lambda j, bi, i: (0, j)),
            pl.BlockSpec((1, tm, tn), lambda j, bi, i: (bi, row0 + i, ga_col0 // tn + j)),
            pl.BlockSpec((1, tm, tn), lambda j, bi, i: (bi, row0 + i, gb_col0 // tn + j)),
        ],
        out_specs=pl.BlockSpec((1, tm, tn), lambda j, bi, i: (bi, i, j)),
        compiler_params=_cparams(("arbitrary", "arbitrary", "arbitrary")),
        name="merge",
    )(y_a, y_b, wa, wb, proj3, proj3)


def _out_proj_kernel(z_ref, w_ref, x_ref, g_ref, o_ref):
    y = _dot(z_ref[0], w_ref[...])
    o_ref[0] = x_ref[0] + g_ref[0] * y


def _out_proj_residual(z, w, x, gate):
    b, t, d = x.shape
    tm, tn = 512, 1024
    return pl.pallas_call(
        _out_proj_kernel,
        out_shape=jax.ShapeDtypeStruct((b, t, d), F32),
        grid=(d // tn, b, t // tm),
        in_specs=[
            pl.BlockSpec((1, tm, d), lambda j, bi, i: (bi, i, 0)),
            pl.BlockSpec((d, tn), lambda j, bi, i: (0, j)),
            pl.BlockSpec((1, tm, tn), lambda j, bi, i: (bi, i, j)),
            pl.BlockSpec((1, 1, tn), lambda j, bi, i: (bi, 0, j)),
        ],
        out_specs=pl.BlockSpec((1, tm, tn), lambda j, bi, i: (bi, i, j)),
        compiler_params=_cparams(("arbitrary", "arbitrary", "arbitrary")),
        name="out_proj",
    )(z, w, x, gate)


def _peer_q_kernel(x_ref, ss_ref, g_ref, w_ref, h_ref, q_ref):
    h = _rms_mod(x_ref[0], g_ref[...], ss_ref[0, 0:1, :], ss_ref[0, 1:2, :]).astype(BF16)
    h_ref[0] = h
    q_ref[0] = _dot(h, w_ref[...])


def _peer_query(x1, ss, g, wq):
    b, t, d = x1.shape
    n = wq.shape[1]
    tm = ROW_TILE
    return pl.pallas_call(
        _peer_q_kernel,
        out_shape=(jax.ShapeDtypeStruct((b, t, d), BF16), jax.ShapeDtypeStruct((b, t, n), F32)),
        grid=(b, t // tm),
        in_specs=[
            pl.BlockSpec((1, tm, d), lambda bi, i: (bi, i, 0)),
            pl.BlockSpec((1, 2, d), lambda bi, i: (bi, 0, 0)),
            pl.BlockSpec((1, d), lambda bi, i: (0, 0)),
            pl.BlockSpec((d, n), lambda bi, i: (0, 0)),
        ],
        out_specs=(pl.BlockSpec((1, tm, d), lambda bi, i: (bi, i, 0)),
                   pl.BlockSpec((1, tm, n), lambda bi, i: (bi, i, 0))),
        compiler_params=_cparams(("arbitrary", "arbitrary")),
        name="peer_query",
    )(x1, ss, g, wq)


def _extract_top(s, n):
    rows = lax.broadcasted_iota(jnp.int32, s.shape, 0)
    big = jnp.int32(s.shape[0])
    vals = []
    for _ in range(n):
        m = jnp.max(s, axis=0, keepdims=True)
        first = jnp.min(jnp.where(s == m, rows, big), axis=0, keepdims=True)
        s = jnp.where(rows == first, NEG_INF, s)
        vals.append(m)
    return vals


def _route_kernel(q_ref, keys_ref, s1_ref, s2_ref, e1_ref, e2_ref, tau_ref):
    q = q_ref[0]
    half = PEER_NKEYS
    s1 = _dot3_nt(keys_ref[0, 0], q[:, :half])
    s2 = _dot3_nt(keys_ref[0, 1], q[:, half:])
    v1 = _extract_top(s1, PEER_TOPK)
    v2 = _extract_top(s2, PEER_TOPK)
    v2s = jnp.concatenate(v2, axis=0)
    cand = jnp.concatenate([v1[i] + v2s for i in range(PEER_TOPK)], axis=0)
    top = _extract_top(cand, PEER_TOPK)
    m = top[0]
    zsum = jnp.zeros_like(m)
    for c in top:
        zsum = zsum + jnp.exp(c - m)
    s1_ref[0, 0] = s1
    s2_ref[0, 0] = s2
    e1_ref[0, 0] = jnp.exp(s1 - v1[0]) / zsum
    e2_ref[0, 0] = jnp.exp(s2 - v2[0])
    tau_ref[0, 0] = top[PEER_TOPK - 1]


def _peer_route(q, keys):
    b, t, _ = q.shape
    tt = 512
    nk = PEER_NKEYS
    big = jax.ShapeDtypeStruct((b, PEER_HEADS, nk, t), F32)
    spec = pl.BlockSpec((1, 1, nk, tt), lambda bi, i, h: (bi, h, 0, i))
    return pl.pallas_call(
        _route_kernel,
        out_shape=(big, big, big, big, jax.ShapeDtypeStruct((b, PEER_HEADS, 1, t), F32)),
        grid=(b, t // tt, PEER_HEADS),
        in_specs=[
            pl.BlockSpec((1, tt, 2 * nk), lambda bi, i, h: (bi, i, h)),
            pl.BlockSpec((1, 2, nk, nk), lambda bi, i, h: (h, 0, 0, 0)),
        ],
        out_specs=(spec, spec, spec, spec,
                   pl.BlockSpec((1, 1, 1, tt), lambda bi, i, h: (bi, h, 0, i))),
        compiler_params=_cparams(("arbitrary", "arbitrary", "arbitrary")),
        name="peer_route",
    )(q, keys)


def _gelu(x):
    return 0.5 * x * (1.0 + lax.erf(x * (2.0 ** -0.5)))


def _peer_mix_kernel(h_ref, u_ref, vt_ref, s1_ref, s2_ref, e1_ref, e2_ref, tau_ref,
                     x_ref, g_ref, o_ref, acc_ref, *, n_a):
    e = pl.program_id(2)
    nk = PEER_NKEYS

    @pl.when(e == 0)
    def _():
        acc_ref[...] = jnp.zeros(acc_ref.shape, F32)

    st = _dot_nt(u_ref[...], h_ref[0])
    parts = []
    for al in range(n_a):
        a = e * n_a + al
        gate = jnp.zeros((nk, st.shape[1]), F32)
        for h in range(PEER_HEADS):
            cand = s1_ref[0, h, pl.ds(a, 1), :] + s2_ref[0, h]
            w = e1_ref[0, h, pl.ds(a, 1), :] * e2_ref[0, h]
            gate = gate + jnp.where(cand >= tau_ref[0, h], w, 0.0)
        act = _gelu(st[al * nk:(al + 1) * nk])
        parts.append((act * gate).astype(BF16))
    a_t = jnp.concatenate(parts, axis=0)
    acc_ref[...] += _dot(vt_ref[...], a_t)

    @pl.when(e == pl.num_programs(2) - 1)
    def _():
        o_ref[0] = x_ref[0] + g_ref[0] * acc_ref[...].T


def _peer_mix(h2, u_bf, vt_bf, s1, s2, e1, e2, tau, x1, gate):
    b, t, d = x1.shape
    nk = PEER_NKEYS
    n_exp = u_bf.shape[0]
    tt = 512
    n_a = 4
    eb = n_a * nk
    rt = pl.BlockSpec((1, PEER_HEADS, nk, tt), lambda bi, i, e: (bi, 0, 0, i))
    return pl.pallas_call(
        functools.partial(_peer_mix_kernel, n_a=n_a),
        out_shape=jax.ShapeDtypeStruct((b, t, d), F32),
        grid=(b, t // tt, n_exp // eb),
        in_specs=[
            pl.BlockSpec((1, tt, d), lambda bi, i, e: (bi, i, 0)),
            pl.BlockSpec((eb, d), lambda bi, i, e: (e, 0)),
            pl.BlockSpec((d, eb), lambda bi, i, e: (0, e)),
            rt, rt, rt, rt,
            pl.BlockSpec((1, PEER_HEADS, 1, tt), lambda bi, i, e: (bi, 0, 0, i)),
            pl.BlockSpec((1, tt, d), lambda bi, i, e: (bi, i, 0)),
            pl.BlockSpec((1, 1, d), lambda bi, i, e: (bi, 0, 0)),
        ],
        out_specs=pl.BlockSpec((1, tt, d), lambda bi, i, e: (bi, i, 0)),
        scratch_shapes=[pltpu.VMEM((d, tt), F32)],
        compiler_params=_cparams(("arbitrary", "arbitrary", "arbitrary")),
        name="peer_mix",
    )(h2, u_bf, vt_bf, s1, s2, e1, e2, tau, x1, gate)


def kernel(x, c, ctx, c_ctx, ada_w, ada_b, norm1_g, norm2_g, w_in, da_qn_g, da_kn_g, da_lam_q1, da_lam_k1, da_lam_q2, da_lam_k2, da_subln_g, gla_w2_f, gla_b_f, gla_w2_b, gla_b_b, gla_norm_g, w_br_a, w_br_b, w_out, peer_wq, peer_keys, peer_u, peer_v):
    b, t, d = x.shape
    t_ctx = ctx.shape[1]
    assert ada_w.shape[0] == 1, "single-layer kernel"
    assert t % ROW_TILE == 0 and t_ctx % ROW_TILE == 0 and t % GRID_W == 0
    l = 0

    cond = jnp.concatenate([c, c_ctx[None, :]], axis=0)
    mod = _ada_modulation(cond, ada_w[l], ada_b[l])[:b + 1].reshape(b + 1, 6, d)
    ss1 = mod[:, 0:2]
    g1 = mod[:b, 2:3]
    ss2 = mod[:b, 3:5]
    g2 = mod[:b, 5:6]

    h_all = _norm_mod_all(x, ctx, ss1, norm1_g[l].reshape(1, d))
    r = t_ctx + t
    wi = w_in[l]
    o_glr = 6144
    w_main = jnp.concatenate([wi[:, :o_glr], wi[:, o_glr + 2 * GLA_RANK:]], axis=1).astype(BF16)
    w_lr = jnp.pad(wi[:, o_glr:o_glr + 2 * GLA_RANK], ((0, 0), (0, LANES - 2 * GLA_RANK))).astype(BF16)
    h2d = h_all.reshape(b * r, d)
    n_main = w_main.shape[1]
    proj3 = _matmul(h2d, w_main, 512, 1024, BF16, "in_proj").reshape(b, r, n_main)
    lr3 = _matmul(h2d, w_lr, 512, LANES, F32, "in_proj_lr").reshape(b, r, LANES)

    cos_t, sa_t, sb_t = _rope_tables(t, t_ctx)
    qm = _qk_prep(proj3, cos_t, sa_t, sb_t, da_qn_g[l], 0, True, t_ctx)
    kh = _qk_prep(proj3, cos_t, sa_t, sb_t, da_kn_g[l], 8, False, t_ctx)
    lam_rows = jnp.zeros((8, LANES), F32)
    lam_rows = lam_rows.at[0, :DA_DH].set(da_lam_q1[l]).at[1, :DA_DH].set(da_lam_k1[l])
    lam_rows = lam_rows.at[2, :DA_DH].set(da_lam_q2[l]).at[3, :DA_DH].set(da_lam_k2[l])
    y_a = _diff_attention(qm, kh, proj3, 16, lam_rows, da_subln_g[l])

    def pad_w2(w2, row0):
        w = w2.reshape(GLA_RANK, GLA_HEADS, GLA_DK).transpose(1, 0, 2)
        return jnp.pad(w, ((0, 0), (row0, LANES - GLA_RANK - row0), (0, 0))).astype(F32)

    cols = (24, 28, 16, 20)
    o_f = _gla_direction(proj3, lr3, pad_w2(gla_w2_f[l], 0),
                         gla_b_f[l].reshape(GLA_HEADS, 1, GLA_DK), t_ctx, cols, False)
    y_b = _gla_direction(proj3, lr3, pad_w2(gla_w2_b[l], GLA_RANK),
                         gla_b_b[l].reshape(GLA_HEADS, 1, GLA_DK), t_ctx, cols, True,
                         o_fwd=o_f, norm_g=gla_norm_g[l])

    z = _merge(y_a, y_b, w_br_a[l].astype(BF16), w_br_b[l].astype(BF16), proj3, 6144, 8192, t_ctx)
    x1 = _out_proj_residual(z, w_out[l].astype(BF16), x, g1)

    h2, q = _peer_query(x1, ss2, norm2_g[l].reshape(1, d), peer_wq[l].astype(BF16))
    s1, s2, e1, e2, tau = _peer_route(q, peer_keys[l])
    return _peer_mix(h2, peer_u[l].astype(BF16), peer_v[l].T.astype(BF16), s1, s2, e1, e2, tau, x1, g2)
```

```python
import functools
import math

import jax
import jax.numpy as jnp
import numpy as np
from jax import lax
from jax.experimental import pallas as pl
from jax.experimental.pallas import tpu as pltpu

F32 = jnp.float32
BF16 = jnp.bfloat16

GRID_W = 64
DA_HEADS = 8
DA_DH = 64
DA_DV = 128
GLA_HEADS = 4
GLA_DK = 128
GLA_DV = 256
GLA_RANK = 16
GLA_TAU = 16.0
GLA_CHUNK = 64
PEER_HEADS = 8
PEER_NKEYS = 128
PEER_TOPK = 16
ROPE_BASE = 10000.0
EPS = 1e-6
LAM_INIT = 0.8 - 0.6 * math.exp(-0.3 * 0)

LANES = 128
ROW_TILE = 256
VMEM_LIMIT = 56 * 1024 * 1024

NEG_INF = float("-inf")
LOG2_E = 1.0 / math.log(2.0)


def _cparams(sem):
    return pltpu.CompilerParams(dimension_semantics=sem, vmem_limit_bytes=VMEM_LIMIT)


def _split2(x):
    hi = x.astype(BF16)
    lo = (x - hi.astype(F32)).astype(BF16)
    return hi, lo


def _split3(x):
    p1 = x.astype(BF16)
    r1 = x - p1.astype(F32)
    p2 = r1.astype(BF16)
    p3 = (r1 - p2.astype(F32)).astype(BF16)
    return p1, p2, p3


def _dot(a, b):
    return jnp.dot(a, b, preferred_element_type=F32)


def _dot_nt(a, b):
    return lax.dot_general(a, b, (((1,), (1,)), ((), ())), preferred_element_type=F32)


def _dot_tn(a, b):
    return lax.dot_general(a, b, (((0,), (0,)), ((), ())), preferred_element_type=F32)


def _dot3_nt(a, b):
    ah, al = _split2(a)
    bh, bl = _split2(b)
    return _dot_nt(ah, bh) + _dot_nt(ah, bl) + _dot_nt(al, bh)


def _dot3(a, b):
    ah, al = _split2(a)
    bh, bl = _split2(b)
    return _dot(ah, bh) + _dot(ah, bl) + _dot(al, bh)


def _exact_lhs_dot(m_bf16, x):
    p1, p2, p3 = _split3(x)
    return _dot(m_bf16, p1) + _dot(m_bf16, p2) + _dot(m_bf16, p3)


def _ada_kernel(cb_ref, w_ref, b_ref, o_ref, *, n_rows):
    tn = w_ref.shape[1]
    o_ref[...] = jnp.zeros(o_ref.shape, F32)
    for r in range(n_rows):
        cv = cb_ref[r]
        s = cv * (1.0 / (1.0 + jnp.exp(-cv)))
        for j in range(tn // LANES):
            sl = slice(j * LANES, (j + 1) * LANES)
            col = jnp.sum(s * w_ref[:, sl], axis=0, keepdims=True)
            o_ref[r:r + 1, sl] = col + b_ref[:, sl]


def _ada_modulation(cond, ada_w, ada_b):
    n_rows, d = cond.shape
    n = ada_w.shape[1]
    tn = 1024
    cb = jnp.broadcast_to(cond[:, :, None], (n_rows, d, LANES))
    return pl.pallas_call(
        functools.partial(_ada_kernel, n_rows=n_rows),
        out_shape=jax.ShapeDtypeStruct((8, n), F32),
        grid=(n // tn,),
        in_specs=[
            pl.BlockSpec((n_rows, d, LANES), lambda j: (0, 0, 0)),
            pl.BlockSpec((d, tn), lambda j: (0, j)),
            pl.BlockSpec((1, tn), lambda j: (0, j)),
        ],
        out_specs=pl.BlockSpec((8, tn), lambda j: (0, j)),
        compiler_params=_cparams(("arbitrary",)),
        name="ada_modulation",
    )(cb, ada_w, ada_b.reshape(1, n))


def _rms_mod(xf, g, shift, scale):
    var = jnp.mean(xf * xf, axis=-1, keepdims=True)
    y = xf * lax.rsqrt(var + EPS) * g
    return y * (1.0 + scale) + shift


def _norm_mod_kernel(x_ref, ctx_ref, ss_ref, g_ref, o_ref, *, nc):
    i = pl.program_id(1)

    def emit(src):
        h = _rms_mod(src[0], g_ref[...], ss_ref[0, 0:1, :], ss_ref[0, 1:2, :])
        o_ref[0] = h.astype(o_ref.dtype)

    @pl.when(i < nc)
    def _():
        emit(ctx_ref)

    @pl.when(i >= nc)
    def _():
        emit(x_ref)


def _norm_mod_all(x, ctx, ss, g):
    b, t, d = x.shape
    tc = ctx.shape[1]
    tm = ROW_TILE
    nc = tc // tm
    r = tc + t
    return pl.pallas_call(
        functools.partial(_norm_mod_kernel, nc=nc),
        out_shape=jax.ShapeDtypeStruct((b, r, d), BF16),
        grid=(b, r // tm),
        in_specs=[
            pl.BlockSpec((1, tm, d), lambda bi, i: (bi, jnp.maximum(i - nc, 0), 0)),
            pl.BlockSpec((1, tm, d), lambda bi, i: (bi, jnp.minimum(i, nc - 1), 0)),
            pl.BlockSpec((1, 2, d), lambda bi, i: (jnp.where(i < nc, b, bi), 0, 0)),
            pl.BlockSpec((1, d), lambda bi, i: (0, 0)),
        ],
        out_specs=pl.BlockSpec((1, tm, d), lambda bi, i: (bi, i, 0)),
        compiler_params=_cparams(("arbitrary", "arbitrary")),
        name="norm_mod",
    )(x, ctx, ss, g)


def _matmul_kernel(a_ref, b_ref, o_ref):
    o_ref[...] = _dot(a_ref[...], b_ref[...]).astype(o_ref.dtype)


def _matmul(a, b, tm, tn, out_dtype, name):
    m, k = a.shape
    n = b.shape[1]
    return pl.pallas_call(
        _matmul_kernel,
        out_shape=jax.ShapeDtypeStruct((m, n), out_dtype),
        grid=(n // tn, m // tm),
        in_specs=[
            pl.BlockSpec((tm, k), lambda j, i: (i, 0)),
            pl.BlockSpec((k, tn), lambda j, i: (0, j)),
        ],
        out_specs=pl.BlockSpec((tm, tn), lambda j, i: (i, j)),
        compiler_params=_cparams(("arbitrary", "arbitrary")),
        name=name,
    )(a, b)


def _rope_tables(t_lat, t_ctx):
    rows = t_lat // GRID_W
    row = np.repeat(np.arange(rows), GRID_W).astype(np.float32)
    col = np.tile(np.arange(GRID_W), rows).astype(np.float32)
    half = DA_DH // 2
    inv = jnp.asarray(ROPE_BASE, F32) ** (-jnp.arange(0, half, 2, dtype=F32) / half)
    ar = jnp.asarray(row)[:, None] * inv
    ac = jnp.asarray(col)[:, None] * inv
    cr, sr, cc, sc = jnp.cos(ar), jnp.sin(ar), jnp.cos(ac), jnp.sin(ac)
    z = jnp.zeros_like(sr)
    cos64 = jnp.concatenate([cr, cr, cc, cc], axis=1)
    sa64 = jnp.concatenate([-sr, z, -sc, z], axis=1)
    sb64 = jnp.concatenate([z, sr, z, sc], axis=1)
    tile2 = lambda a: jnp.concatenate([a, a], axis=1)
    pad = lambda a, v: jnp.concatenate([jnp.full((t_ctx, LANES), v, F32), tile2(a)], axis=0)
    return pad(cos64, 1.0), pad(sa64, 0.0), pad(sb64, 0.0)


def _qk_prep_kernel(x_ref, cos_ref, sa_ref, sb_ref, g_ref, bd_ref, o_ref, *, is_q):
    x = x_ref[0].astype(F32)
    ss = _exact_lhs_dot_rhs(x * x, bd_ref[...])
    y = x * lax.rsqrt(ss * (1.0 / DA_DH) + EPS) * g_ref[0]
    up = pltpu.roll(y, LANES - 16, axis=1)
    dn = pltpu.roll(y, 16, axis=1)
    o = y * cos_ref[...] + up * sa_ref[...] + dn * sb_ref[...]
    if is_q:
        o = o * (DA_DH ** -0.5 * LOG2_E)
        lane = lax.broadcasted_iota(jnp.int32, o.shape, 1)
        zero = jnp.zeros_like(o)
        o_ref[0, :, :LANES] = jnp.where(lane < DA_DH, o, zero).astype(o_ref.dtype)
        o_ref[0, :, LANES:] = jnp.where(lane >= DA_DH, o, zero).astype(o_ref.dtype)
    else:
        o_ref[0] = o.astype(o_ref.dtype)


def _exact_lhs_dot_rhs(x, m_bf16):
    p1, p2, p3 = _split3(x)
    return _dot(p1, m_bf16) + _dot(p2, m_bf16) + _dot(p3, m_bf16)


def _qk_prep(proj3, cos_t, sa_t, sb_t, gain, col0, is_q, t_ctx):
    b, r, _ = proj3.shape
    tm = ROW_TILE
    row0 = t_ctx // tm if is_q else 0
    n_rows = r - t_ctx if is_q else r
    width = 2 * LANES if is_q else LANES
    bd = jnp.asarray(np.kron(np.eye(2), np.ones((DA_DH, DA_DH))), BF16)
    g2 = jnp.concatenate([gain, gain]).reshape(1, 1, LANES).astype(F32)
    return pl.pallas_call(
        functools.partial(_qk_prep_kernel, is_q=is_q),
        out_shape=jax.ShapeDtypeStruct((b, n_rows, DA_HEADS * width), BF16),
        grid=(b, n_rows // tm, DA_HEADS),
        in_specs=[
            pl.BlockSpec((1, tm, LANES), lambda bi, i, h: (bi, row0 + i, col0 + h)),
            pl.BlockSpec((tm, LANES), lambda bi, i, h: (row0 + i, 0)),
            pl.BlockSpec((tm, LANES), lambda bi, i, h: (row0 + i, 0)),
            pl.BlockSpec((tm, LANES), lambda bi, i, h: (row0 + i, 0)),
            pl.BlockSpec((1, 1, LANES), lambda bi, i, h: (0, 0, 0)),
            pl.BlockSpec((LANES, LANES), lambda bi, i, h: (0, 0)),
        ],
        out_specs=pl.BlockSpec((1, tm, width), lambda bi, i, h: (bi, i, h)),
        compiler_params=_cparams(("arbitrary", "arbitrary", "arbitrary")),
        name="q_prep" if is_q else "k_prep",
    )(proj3, cos_t, sa_t, sb_t, g2, bd)


def _attn_kernel(q_ref, k_ref, v_ref, lam_ref, g_ref, o_ref, s_ref, *, tk, nk):
    tq = q_ref.shape[1]
    q = q_ref[0]
    qs = jnp.concatenate([q[:, :LANES], q[:, LANES:]], axis=0)
    mpart = None
    for c in range(nk):
        s = _dot_nt(qs, k_ref[0, c * tk:(c + 1) * tk, :])
        s_ref[:, c * tk:(c + 1) * tk] = s
        for j in range(tk // LANES):
            blk = s[:, j * LANES:(j + 1) * LANES]
            mpart = blk if mpart is None else jnp.maximum(mpart, blk)
    m = jnp.max(mpart, axis=1, keepdims=True)
    ones = jnp.ones((tk, LANES), BF16)
    acc = jnp.zeros((2 * tq, 2 * LANES), F32)
    for c in range(nk):
        p = jnp.exp2((s_ref[:, c * tk:(c + 1) * tk] - m).astype(BF16))
        vaug = jnp.concatenate([v_ref[0, c * tk:(c + 1) * tk, :], ones], axis=1)
        acc = acc + _dot(p, vaug)
    ratio = acc[:, :LANES] / acc[:, LANES:]
    lam = (jnp.exp(jnp.sum(lam_ref[0:1, :] * lam_ref[1:2, :], axis=1, keepdims=True))
           - jnp.exp(jnp.sum(lam_ref[2:3, :] * lam_ref[3:4, :], axis=1, keepdims=True)) + LAM_INIT)
    o = ratio[:tq] - lam * ratio[tq:]
    var = jnp.mean(o * o, axis=-1, keepdims=True)
    o = o * lax.rsqrt(var + EPS) * g_ref[...] * (1.0 - LAM_INIT)
    o_ref[0] = o.astype(o_ref.dtype)


def _diff_attention(qm, kh, proj3, v_col0, lam_rows, subln_g):
    b, t, _ = qm.shape
    r = kh.shape[1]
    tq = ROW_TILE
    tk = ROW_TILE
    return pl.pallas_call(
        functools.partial(_attn_kernel, tk=tk, nk=r // tk),
        out_shape=jax.ShapeDtypeStruct((b, t, DA_HEADS * DA_DV), BF16),
        grid=(b, DA_HEADS, t // tq),
        in_specs=[
            pl.BlockSpec((1, tq, 2 * LANES), lambda bi, h, i: (bi, i, h)),
            pl.BlockSpec((1, r, LANES), lambda bi, h, i: (bi, 0, h)),
            pl.BlockSpec((1, r, LANES), lambda bi, h, i: (bi, 0, v_col0 + h)),
            pl.BlockSpec((8, LANES), lambda bi, h, i: (0, 0)),
            pl.BlockSpec((1, LANES), lambda bi, h, i: (0, 0)),
        ],
        out_specs=pl.BlockSpec((1, tq, DA_DV), lambda bi, h, i: (bi, i, h)),
        scratch_shapes=[pltpu.VMEM((2 * tq, r), F32)],
        compiler_params=_cparams(("arbitrary", "arbitrary", "arbitrary")),
        name="diff_attention",
    )(qm, kh, proj3, lam_rows, subln_g.reshape(1, DA_DV).astype(F32))


def _log_sigmoid(z):
    return jnp.minimum(z, 0.0) - jnp.log(1.0 + jnp.exp(-jnp.abs(z)))


def _gla_kernel(*refs, reverse, final, n_chunks):
    if final:
        (q_ref, k_ref, v_ref, lr_ref, w2_ref, b_ref, tri_ref, ones_ref,
         of_ref, gr_ref, ng_ref, o_ref, st_ref) = refs
    else:
        (q_ref, k_ref, v_ref, lr_ref, w2_ref, b_ref, tri_ref, ones_ref, o_ref, st_ref) = refs
    c_len = GLA_CHUNK

    @pl.when(pl.program_id(2) == 0)
    def _():
        st_ref[...] = jnp.zeros(st_ref.shape, F32)

    q = q_ref[0].astype(F32) * (GLA_DK ** -0.5)
    k = k_ref[0].astype(F32)
    v = v_ref[0]
    z = _dot3(lr_ref[0], w2_ref[0]) + b_ref[0]
    la = _log_sigmoid(z) * (1.0 / GLA_TAU)
    tri = tri_ref[...]
    cum = _exact_lhs_dot(tri, la)
    tot = _exact_lhs_dot(ones_ref[...], la)
    q_in = (q * jnp.exp(cum)).astype(BF16)
    k_in = (k * jnp.exp(-cum)).astype(BF16)
    k_st = (k * jnp.exp(tot - cum)).astype(BF16)
    dec = jnp.exp(tot)
    att = jnp.where(tri > 0, _dot_nt(q_in, k_in), 0.0)
    o_intra = _dot(att.astype(BF16), v)

    order = range(n_chunks - 1, -1, -1) if reverse else range(n_chunks)
    outs = [None] * n_chunks
    for c in order:
        sl = slice(c * c_len, (c + 1) * c_len)
        st = st_ref[...]
        outs[c] = o_intra[sl] + _dot_nt(q_in[sl], st.astype(BF16))
        kv = _dot_tn(v[sl], k_st[sl])
        st_ref[...] = st * dec[c * c_len:c * c_len + 1, :] + kv
    o = jnp.concatenate(outs, axis=0)
    if final:
        o = o + of_ref[0].astype(F32)
        var = jnp.mean(o * o, axis=-1, keepdims=True)
        y = o * lax.rsqrt(var + EPS) * ng_ref[...]
        gr = gr_ref[0].astype(F32)
        y = y * (gr * (1.0 / (1.0 + jnp.exp(-gr))))
        o_ref[0] = y.astype(o_ref.dtype)
    else:
        o_ref[0] = o.astype(o_ref.dtype)


def _gla_direction(proj3, lr3, w2p, bias, t_ctx, cols, reverse, o_fwd=None, norm_g=None):
    b, r, _ = proj3.shape
    tc = ROW_TILE
    n_chunks = tc // GLA_CHUNK
    ng = r // tc
    ncg = t_ctx // tc
    final = o_fwd is not None
    q_col, k_col, v_col, gr_col = cols

    blk = np.kron(np.eye(n_chunks), np.ones((GLA_CHUNK, GLA_CHUNK)))
    full_tri = np.tril(np.ones((tc, tc)))
    tri = blk * (full_tri.T if reverse else full_tri)
    tri = jnp.asarray(tri, BF16)
    ones_bd = jnp.asarray(blk, BF16)

    if reverse:
        def grp(s):
            return jnp.where(s < ncg, ncg - 1 - s, ng - 1 - (s - ncg))
    else:
        def grp(s):
            return s

    def lat(s):
        return jnp.maximum(grp(jnp.maximum(s, ncg)) - ncg, 0)

    in_specs = [
        pl.BlockSpec((1, tc, GLA_DK), lambda bi, h, s: (bi, grp(s), q_col + h)),
        pl.BlockSpec((1, tc, GLA_DK), lambda bi, h, s: (bi, grp(s), k_col + h)),
        pl.BlockSpec((1, tc, GLA_DV), lambda bi, h, s: (bi, grp(s), v_col + h)),
        pl.BlockSpec((1, tc, LANES), lambda bi, h, s: (bi, grp(s), 0)),
        pl.BlockSpec((1, LANES, GLA_DK), lambda bi, h, s: (h, 0, 0)),
        pl.BlockSpec((1, 1, GLA_DK), lambda bi, h, s: (h, 0, 0)),
        pl.BlockSpec((tc, tc), lambda bi, h, s: (0, 0)),
        pl.BlockSpec((tc, tc), lambda bi, h, s: (0, 0)),
    ]
    args = [proj3, proj3, proj3, lr3, w2p, bias, tri, ones_bd]
    if final:
        in_specs += [
            pl.BlockSpec((1, tc, GLA_DV), lambda bi, h, s: (bi, lat(s), h)),
            pl.BlockSpec((1, tc, GLA_DV), lambda bi, h, s: (bi, grp(s), gr_col + h)),
            pl.BlockSpec((1, GLA_DV), lambda bi, h, s: (0, 0)),
        ]
        args += [o_fwd, proj3, norm_g.reshape(1, GLA_DV).astype(F32)]
    return pl.pallas_call(
        functools.partial(_gla_kernel, reverse=reverse, final=final, n_chunks=n_chunks),
        out_shape=jax.ShapeDtypeStruct((b, r - t_ctx, GLA_HEADS * GLA_DV), BF16 if final else F32),
        grid=(b, GLA_HEADS, ng),
        in_specs=in_specs,
        out_specs=pl.BlockSpec((1, tc, GLA_DV), lambda bi, h, s: (bi, lat(s), h)),
        scratch_shapes=[pltpu.VMEM((GLA_DV, GLA_DK), F32)],
        compiler_params=_cparams(("arbitrary", "arbitrary", "arbitrary")),
        name="gla_bwd" if reverse else "gla_fwd",
    )(*args)


def _sigmoid(x):
    return 1.0 / (1.0 + jnp.exp(-x))


def _merge_kernel(ya_ref, yb_ref, wa_ref, wb_ref, ga_ref, gb_ref, o_ref):
    a = _dot(ya_ref[0], wa_ref[...])
    bb = _dot(yb_ref[0], wb_ref[...])
    z = _sigmoid(ga_ref[0].astype(F32)) * a + _sigmoid(gb_ref[0].astype(F32)) * bb
    o_ref[0] = z.astype(o_ref.dtype)


def _merge(y_a, y_b, wa, wb, proj3, ga_col0, gb_col0, t_ctx):
    b, t, ka = y_a.shape
    kb = y_b.shape[2]
    d = wa.shape[1]
    tm, tn = ROW_TILE, 1024
    row0 = t_ctx // tm
    return pl.pallas_call(
        _merge_kernel,
        out_shape=jax.ShapeDtypeStruct((b, t, d), BF16),
        grid=(d // tn, b, t // tm),
        in_specs=[
            pl.BlockSpec((1, tm, ka), lambda j, bi, i: (bi, i, 0)),
            pl.BlockSpec((1, tm, kb), lambda j, bi, i: (bi, i, 0)),
            pl.BlockSpec((ka, tn), lambda j, bi, i: (0, j)),
            pl.BlockSpec((kb, tn), lambda j, bi, i: (0, j)),
            pl.BlockSpec((1, tm, tn), lambda j, bi, i: (bi, row0 + i, ga_col0 // tn + j)),
            pl.BlockSpec((1, tm, tn), lambda j, bi, i: (bi, row0 + i, gb_col0 // tn + j)),
        ],
        out_specs=pl.BlockSpec((1, tm, tn), lambda j, bi, i: (bi, i, j)),
        compiler_params=_cparams(("arbitrary", "arbitrary", "arbitrary")),
        name="merge",
    )(y_a, y_b, wa, wb, proj3, proj3)


def _out_proj_kernel(z_ref, w_ref, x_ref, g_ref, o_ref):
    y = _dot(z_ref[0], w_ref[...])
    o_ref[0] = x_ref[0] + g_ref[0] * y


def _out_proj_residual(z, w, x, gate):
    b, t, d = x.shape
    tm, tn = 512, 1024
    return pl.pallas_call(
        _out_proj_kernel,
        out_shape=jax.ShapeDtypeStruct((b, t, d), F32),
        grid=(d // tn, b, t // tm),
        in_specs=[
            pl.BlockSpec((1, tm, d), lambda j, bi, i: (bi, i, 0)),
            pl.BlockSpec((d, tn), lambda j, bi, i: (0, j)),
            pl.BlockSpec((1, tm, tn), lambda j, bi, i: (bi, i, j)),
            pl.BlockSpec((1, 1, tn), lambda j, bi, i: (bi, 0, j)),
        ],
        out_specs=pl.BlockSpec((1, tm, tn), lambda j, bi, i: (bi, i, j)),
        compiler_params=_cparams(("arbitrary", "arbitrary", "arbitrary")),
        name="out_proj",
    )(z, w, x, gate)


def _peer_q_kernel(x_ref, ss_ref, g_ref, w_ref, h_ref, q_ref):
    h = _rms_mod(x_ref[0], g_ref[...], ss_ref[0, 0:1, :], ss_ref[0, 1:2, :]).astype(BF16)
    h_ref[0] = h
    q_ref[0] = _dot(h, w_ref[...])


def _peer_query(x1, ss, g, wq):
    b, t, d = x1.shape
    n = wq.shape[1]
    tm = ROW_TILE
    return pl.pallas_call(
        _peer_q_kernel,
        out_shape=(jax.ShapeDtypeStruct((b, t, d), BF16), jax.ShapeDtypeStruct((b, t, n), F32)),
        grid=(b, t // tm),
        in_specs=[
            pl.BlockSpec((1, tm, d), lambda bi, i: (bi, i, 0)),
            pl.BlockSpec((1, 2, d), lambda bi, i: (bi, 0, 0)),
            pl.BlockSpec((1, d), lambda bi, i: (0, 0)),
            pl.BlockSpec((d, n), lambda bi, i: (0, 0)),
        ],
        out_specs=(pl.BlockSpec((1, tm, d), lambda bi, i: (bi, i, 0)),
                   pl.BlockSpec((1, tm, n), lambda bi, i: (bi, i, 0))),
        compiler_params=_cparams(("arbitrary", "arbitrary")),
        name="peer_query",
    )(x1, ss, g, wq)


SUBLANES = 8


def _cmp_exchange(xs, hi, lo):
    a, b = xs[hi], xs[lo]
    if a is None and b is None:
        return
    if a is None or b is None:
        xs[hi], xs[lo] = (b if a is None else a), None
        return
    xs[hi], xs[lo] = jnp.maximum(a, b), jnp.minimum(a, b)


def _bitonic_sort_desc(xs):
    n = len(xs)
    xs = list(xs)
    k = 2
    while k <= n:
        j = k // 2
        while j >= 1:
            for i in range(n):
                l = i ^ j
                if l > i:
                    if (i & k) == 0:
                        _cmp_exchange(xs, i, l)
                    else:
                        _cmp_exchange(xs, l, i)
            j //= 2
        k *= 2
    return xs


def _bitonic_merge_desc(xs):
    n = len(xs)
    xs = list(xs)
    j = n // 2
    while j >= 1:
        for i in range(n):
            l = i ^ j
            if l > i:
                _cmp_exchange(xs, i, l)
        j //= 2
    return xs


def _top16_sorted(xs):
    n = PEER_TOPK
    xs = list(xs) + [None] * (n - len(xs))
    xs = _bitonic_sort_desc(xs)
    shift = SUBLANES // 2
    while shift >= 1:
        merged = []
        for i in range(n):
            a, b = xs[i], xs[n - 1 - i]
            if b is not None:
                b = pltpu.roll(b, shift, axis=0)
            merged.append(b if a is None else (a if b is None else jnp.maximum(a, b)))
        xs = _bitonic_merge_desc(merged)
        shift //= 2
    return xs


def _route_kernel(q_ref, keys_ref, s1_ref, s2_ref, e1_ref, e2_ref, tau_ref):
    q = q_ref[0]
    nk = PEER_NKEYS
    s1 = _dot3_nt(keys_ref[0, 0], q[:, :nk])
    s2 = _dot3_nt(keys_ref[0, 1], q[:, nk:])
    split = lambda s: [s[i * SUBLANES:(i + 1) * SUBLANES] for i in range(nk // SUBLANES)]
    v1 = _top16_sorted(split(s1))
    v2 = _top16_sorted(split(s2))
    sub = lax.broadcasted_iota(jnp.int32, v1[0].shape, 0)

    def spread(vals):
        out = vals[0]
        for j in range(1, SUBLANES):
            out = jnp.where(sub == j, vals[j], out)
        return out

    v2_lo, v2_hi, v1_hi = spread(v2[:SUBLANES]), spread(v2[SUBLANES:]), spread(v1[SUBLANES:])
    cands = [v1[0] + v2_lo, v1[0] + v2_hi]
    cands += [v1[i] + v2_lo for i in range(1, SUBLANES)]
    cands += [v1_hi + v2[0]]
    top = _top16_sorted(cands)
    m = top[0]
    zsum = jnp.zeros_like(m)
    for c in top:
        zsum = zsum + jnp.exp(c - m)
    s1_ref[0, 0] = s1
    s2_ref[0, 0] = s2
    e1_ref[0, 0] = jnp.exp(s1 - v1[0][0:1]) / zsum[0:1]
    e2_ref[0, 0] = jnp.exp(s2 - v2[0][0:1])
    tau_ref[0, 0] = top[PEER_TOPK - 1][0:1]


def _peer_route(q, keys):
    b, t, _ = q.shape
    tt = 512
    nk = PEER_NKEYS
    big = jax.ShapeDtypeStruct((b, PEER_HEADS, nk, t), F32)
    spec = pl.BlockSpec((1, 1, nk, tt), lambda bi, i, h: (bi, h, 0, i))
    return pl.pallas_call(
        _route_kernel,
        out_shape=(big, big, big, big, jax.ShapeDtypeStruct((b, PEER_HEADS, 1, t), F32)),
        grid=(b, t // tt, PEER_HEADS),
        in_specs=[
            pl.BlockSpec((1, tt, 2 * nk), lambda bi, i, h: (bi, i, h)),
            pl.BlockSpec((1, 2, nk, nk), lambda bi, i, h: (h, 0, 0, 0)),
        ],
        out_specs=(spec, spec, spec, spec,
                   pl.BlockSpec((1, 1, 1, tt), lambda bi, i, h: (bi, h, 0, i))),
        compiler_params=_cparams(("arbitrary", "arbitrary", "arbitrary")),
        name="peer_route",
    )(q, keys)


def _gelu(x):
    return 0.5 * x * (1.0 + lax.erf(x * (2.0 ** -0.5)))


def _peer_mix_kernel(h_ref, u_ref, vt_ref, s1_ref, s2_ref, e1_ref, e2_ref, tau_ref,
                     o_ref, st_ref, gt_ref, *, n_a, n_e):
    f = pl.program_id(0)
    nk = PEER_NKEYS
    slot_p = lax.rem(f, 2)
    slot_c = 1 - slot_p
    e_c = lax.rem(jnp.maximum(f - 1, 0), n_e)

    @pl.when(f == 0)
    def _():
        st_ref[1] = jnp.zeros(st_ref.shape[1:], F32)
        gt_ref[1] = jnp.zeros(gt_ref.shape[1:], F32)

    @pl.when(e_c == 0)
    def _():
        o_ref[...] = jnp.zeros(o_ref.shape, F32)

    a_t = (_gelu(st_ref[slot_c]) * gt_ref[slot_c]).astype(BF16)
    o_ref[0] += _dot(vt_ref[...], a_t)

    st_ref[slot_p] = _dot_nt(u_ref[...], h_ref[0])
    for al in range(n_a):
        gate = None
        for h in range(PEER_HEADS):
            cand = s1_ref[0, h, al:al + 1, :] + s2_ref[0, h]
            w = e1_ref[0, h, al:al + 1, :] * e2_ref[0, h]
            term = jnp.where(cand >= tau_ref[0, h], w, 0.0)
            gate = term if gate is None else gate + term
        gt_ref[slot_p, al * nk:(al + 1) * nk, :] = gate


def _peer_mix(h2, u_bf, vt_bf, s1, s2, e1, e2, tau):
    b, t, d = h2.shape
    nk = PEER_NKEYS
    n_exp = u_bf.shape[0]
    tt = 512
    n_a = 8
    eb = n_a * nk
    n_e = n_exp // eb
    n_t = t // tt
    total = b * n_t * n_e

    def split(idx):
        return idx // (n_t * n_e), lax.rem(idx // n_e, n_t), lax.rem(idx, n_e)

    prod = lambda f: split(jnp.minimum(f, total - 1))
    cons = lambda f: split(jnp.maximum(f - 1, 0))

    def tok_spec(rows):
        def imap(f):
            bi, ti, _ = prod(f)
            return (bi, 0, 0, ti)
        return pl.BlockSpec((1, PEER_HEADS, rows, tt), imap)

    def blk_spec():
        def imap(f):
            bi, ti, e = prod(f)
            return (bi, 0, e, ti)
        return pl.BlockSpec((1, PEER_HEADS, n_a, tt), imap)

    return pl.pallas_call(
        functools.partial(_peer_mix_kernel, n_a=n_a, n_e=n_e),
        out_shape=jax.ShapeDtypeStruct((b, d, t), F32),
        grid=(total + 1,),
        in_specs=[
            pl.BlockSpec((1, tt, d), lambda f: (prod(f)[0], prod(f)[1], 0)),
            pl.BlockSpec((eb, d), lambda f: (prod(f)[2], 0)),
            pl.BlockSpec((d, eb), lambda f: (0, cons(f)[2])),
            blk_spec(), tok_spec(nk), blk_spec(), tok_spec(nk), tok_spec(1),
        ],
        out_specs=pl.BlockSpec((1, d, tt), lambda f: (cons(f)[0], 0, cons(f)[1])),
        scratch_shapes=[pltpu.VMEM((2, eb, tt), F32), pltpu.VMEM((2, eb, tt), F32)],
        compiler_params=_cparams(("arbitrary",)),
        name="peer_mix",
    )(h2, u_bf, vt_bf, s1, s2, e1, e2, tau)


def _peer_out_kernel(acc_ref, x_ref, g_ref, o_ref):
    o_ref[0] = x_ref[0] + g_ref[0] * acc_ref[0].T


def _peer_residual(acc_t, x1, gate):
    b, t, d = x1.shape
    tt = 512
    return pl.pallas_call(
        _peer_out_kernel,
        out_shape=jax.ShapeDtypeStruct((b, t, d), F32),
        grid=(b, t // tt),
        in_specs=[
            pl.BlockSpec((1, d, tt), lambda bi, i: (bi, 0, i)),
            pl.BlockSpec((1, tt, d), lambda bi, i: (bi, i, 0)),
            pl.BlockSpec((1, 1, d), lambda bi, i: (bi, 0, 0)),
        ],
        out_specs=pl.BlockSpec((1, tt, d), lambda bi, i: (bi, i, 0)),
        compiler_params=_cparams(("arbitrary", "arbitrary")),
        name="peer_residual",
    )(acc_t, x1, gate)


def kernel(x, c, ctx, c_ctx, ada_w, ada_b, norm1_g, norm2_g, w_in, da_qn_g, da_kn_g, da_lam_q1, da_lam_k1, da_lam_q2, da_lam_k2, da_subln_g, gla_w2_f, gla_b_f, gla_w2_b, gla_b_b, gla_norm_g, w_br_a, w_br_b, w_out, peer_wq, peer_keys, peer_u, peer_v):
    b, t, d = x.shape
    t_ctx = ctx.shape[1]
    assert ada_w.shape[0] == 1, "single-layer kernel"
    assert t % ROW_TILE == 0 and t_ctx % ROW_TILE == 0 and t % GRID_W == 0
    l = 0

    cond = jnp.concatenate([c, c_ctx[None, :]], axis=0)
    mod = _ada_modulation(cond, ada_w[l], ada_b[l])[:b + 1].reshape(b + 1, 6, d)
    ss1 = mod[:, 0:2]
    g1 = mod[:b, 2:3]
    ss2 = mod[:b, 3:5]
    g2 = mod[:b, 5:6]

    h_all = _norm_mod_all(x, ctx, ss1, norm1_g[l].reshape(1, d))
    r = t_ctx + t
    wi = w_in[l]
    o_glr = 6144
    w_main = jnp.concatenate([wi[:, :o_glr], wi[:, o_glr + 2 * GLA_RANK:]], axis=1).astype(BF16)
    w_lr = jnp.pad(wi[:, o_glr:o_glr + 2 * GLA_RANK], ((0, 0), (0, LANES - 2 * GLA_RANK))).astype(BF16)
    h2d = h_all.reshape(b * r, d)
    n_main = w_main.shape[1]
    proj3 = _matmul(h2d, w_main, 512, 1024, BF16, "in_proj").reshape(b, r, n_main)
    lr3 = _matmul(h2d, w_lr, 512, LANES, F32, "in_proj_lr").reshape(b, r, LANES)

    cos_t, sa_t, sb_t = _rope_tables(t, t_ctx)
    qm = _qk_prep(proj3, cos_t, sa_t, sb_t, da_qn_g[l], 0, True, t_ctx)
    kh = _qk_prep(proj3, cos_t, sa_t, sb_t, da_kn_g[l], 8, False, t_ctx)
    lam_rows = jnp.zeros((8, LANES), F32)
    lam_rows = lam_rows.at[0, :DA_DH].set(da_lam_q1[l]).at[1, :DA_DH].set(da_lam_k1[l])
    lam_rows = lam_rows.at[2, :DA_DH].set(da_lam_q2[l]).at[3, :DA_DH].set(da_lam_k2[l])
    y_a = _diff_attention(qm, kh, proj3, 16, lam_rows, da_subln_g[l])

    def pad_w2(w2, row0):
        w = w2.reshape(GLA_RANK, GLA_HEADS, GLA_DK).transpose(1, 0, 2)
        return jnp.pad(w, ((0, 0), (row0, LANES - GLA_RANK - row0), (0, 0))).astype(F32)

    cols = (24, 28, 16, 20)
    o_f = _gla_direction(proj3, lr3, pad_w2(gla_w2_f[l], 0),
                         gla_b_f[l].reshape(GLA_HEADS, 1, GLA_DK), t_ctx, cols, False)
    y_b = _gla_direction(proj3, lr3, pad_w2(gla_w2_b[l], GLA_RANK),
                         gla_b_b[l].reshape(GLA_HEADS, 1, GLA_DK), t_ctx, cols, True,
                         o_fwd=o_f, norm_g=gla_norm_g[l])

    z = _merge(y_a, y_b, w_br_a[l].astype(BF16), w_br_b[l].astype(BF16), proj3, 6144, 8192, t_ctx)
    x1 = _out_proj_residual(z, w_out[l].astype(BF16), x, g1)

    h2, q = _peer_query(x1, ss2, norm2_g[l].reshape(1, d), peer_wq[l].astype(BF16))
    s1, s2, e1, e2, tau = _peer_route(q, peer_keys[l])
    acc_t = _peer_mix(h2, peer_u[l].astype(BF16), peer_v[l].T.astype(BF16), s1, s2, e1, e2, tau)
    return _peer_residual(acc_t, x1, g2)
```

```python
import functools
import math

import jax
import jax.numpy as jnp
import numpy as np
from jax import lax
from jax.experimental import pallas as pl
from jax.experimental.pallas import tpu as pltpu

F32 = jnp.float32
BF16 = jnp.bfloat16

GRID_W = 64
DA_HEADS = 8
DA_DH = 64
DA_DV = 128
GLA_HEADS = 4
GLA_DK = 128
GLA_DV = 256
GLA_RANK = 16
GLA_TAU = 16.0
GLA_CHUNK = 64
PEER_HEADS = 8
PEER_NKEYS = 128
PEER_TOPK = 16
ROPE_BASE = 10000.0
EPS = 1e-6
LAM_INIT = 0.8 - 0.6 * math.exp(-0.3 * 0)

LANES = 128
ROW_TILE = 256
VMEM_LIMIT = 56 * 1024 * 1024

NEG_INF = float("-inf")
LOG2_E = 1.0 / math.log(2.0)


def _cparams(sem):
    return pltpu.CompilerParams(dimension_semantics=sem, vmem_limit_bytes=VMEM_LIMIT)


def _split2(x):
    hi = x.astype(BF16)
    lo = (x - hi.astype(F32)).astype(BF16)
    return hi, lo


def _split3(x):
    p1 = x.astype(BF16)
    r1 = x - p1.astype(F32)
    p2 = r1.astype(BF16)
    p3 = (r1 - p2.astype(F32)).astype(BF16)
    return p1, p2, p3


def _dot(a, b):
    return jnp.dot(a, b, preferred_element_type=F32)


def _dot_nt(a, b):
    return lax.dot_general(a, b, (((1,), (1,)), ((), ())), preferred_element_type=F32)


def _dot_tn(a, b):
    return lax.dot_general(a, b, (((0,), (0,)), ((), ())), preferred_element_type=F32)


def _dot3_nt(a, b):
    ah, al = _split2(a)
    bh, bl = _split2(b)
    return _dot_nt(ah, bh) + _dot_nt(ah, bl) + _dot_nt(al, bh)


def _dot3(a, b):
    ah, al = _split2(a)
    bh, bl = _split2(b)
    return _dot(ah, bh) + _dot(ah, bl) + _dot(al, bh)


def _exact_lhs_dot(m_bf16, x):
    p1, p2, p3 = _split3(x)
    return _dot(m_bf16, p1) + _dot(m_bf16, p2) + _dot(m_bf16, p3)


def _ada_kernel(cb_ref, w_ref, b_ref, o_ref, *, n_rows):
    tn = w_ref.shape[1]
    o_ref[...] = jnp.zeros(o_ref.shape, F32)
    for r in range(n_rows):
        cv = cb_ref[r]
        s = cv * (1.0 / (1.0 + jnp.exp(-cv)))
        for j in range(tn // LANES):
            sl = slice(j * LANES, (j + 1) * LANES)
            col = jnp.sum(s * w_ref[:, sl], axis=0, keepdims=True)
            o_ref[r:r + 1, sl] = col + b_ref[:, sl]


def _ada_modulation(cond, ada_w, ada_b):
    n_rows, d = cond.shape
    n = ada_w.shape[1]
    tn = 1024
    cb = jnp.broadcast_to(cond[:, :, None], (n_rows, d, LANES))
    return pl.pallas_call(
        functools.partial(_ada_kernel, n_rows=n_rows),
        out_shape=jax.ShapeDtypeStruct((8, n), F32),
        grid=(n // tn,),
        in_specs=[
            pl.BlockSpec((n_rows, d, LANES), lambda j: (0, 0, 0)),
            pl.BlockSpec((d, tn), lambda j: (0, j)),
            pl.BlockSpec((1, tn), lambda j: (0, j)),
        ],
        out_specs=pl.BlockSpec((8, tn), lambda j: (0, j)),
        compiler_params=_cparams(("arbitrary",)),
        name="ada_modulation",
    )(cb, ada_w, ada_b.reshape(1, n))


def _rms_mod(xf, g, shift, scale):
    var = jnp.mean(xf * xf, axis=-1, keepdims=True)
    y = xf * lax.rsqrt(var + EPS) * g
    return y * (1.0 + scale) + shift


def _norm_mod_kernel(x_ref, ctx_ref, ss_ref, g_ref, o_ref, *, nc):
    i = pl.program_id(1)

    def emit(src):
        h = _rms_mod(src[0], g_ref[...], ss_ref[0, 0:1, :], ss_ref[0, 1:2, :])
        o_ref[0] = h.astype(o_ref.dtype)

    @pl.when(i < nc)
    def _():
        emit(ctx_ref)

    @pl.when(i >= nc)
    def _():
        emit(x_ref)


def _norm_mod_all(x, ctx, ss, g):
    b, t, d = x.shape
    tc = ctx.shape[1]
    tm = ROW_TILE
    nc = tc // tm
    r = tc + t
    return pl.pallas_call(
        functools.partial(_norm_mod_kernel, nc=nc),
        out_shape=jax.ShapeDtypeStruct((b, r, d), BF16),
        grid=(b, r // tm),
        in_specs=[
            pl.BlockSpec((1, tm, d), lambda bi, i: (bi, jnp.maximum(i - nc, 0), 0)),
            pl.BlockSpec((1, tm, d), lambda bi, i: (bi, jnp.minimum(i, nc - 1), 0)),
            pl.BlockSpec((1, 2, d), lambda bi, i: (jnp.where(i < nc, b, bi), 0, 0)),
            pl.BlockSpec((1, d), lambda bi, i: (0, 0)),
        ],
        out_specs=pl.BlockSpec((1, tm, d), lambda bi, i: (bi, i, 0)),
        compiler_params=_cparams(("arbitrary", "arbitrary")),
        name="norm_mod",
    )(x, ctx, ss, g)


def _matmul_kernel(a_ref, b_ref, o_ref):
    o_ref[...] = _dot(a_ref[...], b_ref[...]).astype(o_ref.dtype)


def _matmul(a, b, tm, tn, out_dtype, name):
    m, k = a.shape
    n = b.shape[1]
    return pl.pallas_call(
        _matmul_kernel,
        out_shape=jax.ShapeDtypeStruct((m, n), out_dtype),
        grid=(n // tn, m // tm),
        in_specs=[
            pl.BlockSpec((tm, k), lambda j, i: (i, 0)),
            pl.BlockSpec((k, tn), lambda j, i: (0, j)),
        ],
        out_specs=pl.BlockSpec((tm, tn), lambda j, i: (i, j)),
        compiler_params=_cparams(("arbitrary", "arbitrary")),
        name=name,
    )(a, b)


def _rope_tables(t_lat, t_ctx):
    rows = t_lat // GRID_W
    row = np.repeat(np.arange(rows), GRID_W).astype(np.float32)
    col = np.tile(np.arange(GRID_W), rows).astype(np.float32)
    half = DA_DH // 2
    inv = jnp.asarray(ROPE_BASE, F32) ** (-jnp.arange(0, half, 2, dtype=F32) / half)
    ar = jnp.asarray(row)[:, None] * inv
    ac = jnp.asarray(col)[:, None] * inv
    cr, sr, cc, sc = jnp.cos(ar), jnp.sin(ar), jnp.cos(ac), jnp.sin(ac)
    z = jnp.zeros_like(sr)
    cos64 = jnp.concatenate([cr, cr, cc, cc], axis=1)
    sa64 = jnp.concatenate([-sr, z, -sc, z], axis=1)
    sb64 = jnp.concatenate([z, sr, z, sc], axis=1)
    tile2 = lambda a: jnp.concatenate([a, a], axis=1)
    pad = lambda a, v: jnp.concatenate([jnp.full((t_ctx, LANES), v, F32), tile2(a)], axis=0)
    return pad(cos64, 1.0), pad(sa64, 0.0), pad(sb64, 0.0)


def _qk_prep_kernel(x_ref, cos_ref, sa_ref, sb_ref, g_ref, bd_ref, o_ref, *, is_q):
    x = x_ref[0].astype(F32)
    ss = _exact_lhs_dot_rhs(x * x, bd_ref[...])
    y = x * lax.rsqrt(ss * (1.0 / DA_DH) + EPS) * g_ref[0]
    up = pltpu.roll(y, LANES - 16, axis=1)
    dn = pltpu.roll(y, 16, axis=1)
    o = y * cos_ref[...] + up * sa_ref[...] + dn * sb_ref[...]
    if is_q:
        o = o * (DA_DH ** -0.5 * LOG2_E)
        lane = lax.broadcasted_iota(jnp.int32, o.shape, 1)
        zero = jnp.zeros_like(o)
        o_ref[0, :, :LANES] = jnp.where(lane < DA_DH, o, zero).astype(o_ref.dtype)
        o_ref[0, :, LANES:] = jnp.where(lane >= DA_DH, o, zero).astype(o_ref.dtype)
    else:
        o_ref[0] = o.astype(o_ref.dtype)


def _exact_lhs_dot_rhs(x, m_bf16):
    p1, p2, p3 = _split3(x)
    return _dot(p1, m_bf16) + _dot(p2, m_bf16) + _dot(p3, m_bf16)


def _qk_prep(proj3, cos_t, sa_t, sb_t, gain, col0, is_q, t_ctx):
    b, r, _ = proj3.shape
    tm = ROW_TILE
    row0 = t_ctx // tm if is_q else 0
    n_rows = r - t_ctx if is_q else r
    width = 2 * LANES if is_q else LANES
    bd = jnp.asarray(np.kron(np.eye(2), np.ones((DA_DH, DA_DH))), BF16)
    g2 = jnp.concatenate([gain, gain]).reshape(1, 1, LANES).astype(F32)
    return pl.pallas_call(
        functools.partial(_qk_prep_kernel, is_q=is_q),
        out_shape=jax.ShapeDtypeStruct((b, n_rows, DA_HEADS * width), BF16),
        grid=(b, n_rows // tm, DA_HEADS),
        in_specs=[
            pl.BlockSpec((1, tm, LANES), lambda bi, i, h: (bi, row0 + i, col0 + h)),
            pl.BlockSpec((tm, LANES), lambda bi, i, h: (row0 + i, 0)),
            pl.BlockSpec((tm, LANES), lambda bi, i, h: (row0 + i, 0)),
            pl.BlockSpec((tm, LANES), lambda bi, i, h: (row0 + i, 0)),
            pl.BlockSpec((1, 1, LANES), lambda bi, i, h: (0, 0, 0)),
            pl.BlockSpec((LANES, LANES), lambda bi, i, h: (0, 0)),
        ],
        out_specs=pl.BlockSpec((1, tm, width), lambda bi, i, h: (bi, i, h)),
        compiler_params=_cparams(("arbitrary", "arbitrary", "arbitrary")),
        name="q_prep" if is_q else "k_prep",
    )(proj3, cos_t, sa_t, sb_t, g2, bd)


def _attn_body(q_ref, k_ref, v_ref, lam_ref, g_ref, o_ref, sp_ref, mp_ref, sc_ref, mc_ref, *, tk, nk):
    tq = q_ref.shape[1]
    q = q_ref[0]
    qs = jnp.concatenate([q[:, :LANES], q[:, LANES:]], axis=0)
    m_c = mc_ref[:, 0:1]
    ones = jnp.ones((tk, LANES), BF16)
    acc = jnp.zeros((2 * tq, 2 * LANES), F32)
    mpart = None
    for c in range(nk):
        sl = slice(c * tk, (c + 1) * tk)
        s = _dot_nt(qs, k_ref[0, sl, :])
        sp_ref[:, sl] = s
        for j in range(tk // LANES):
            blk = s[:, j * LANES:(j + 1) * LANES]
            mpart = blk if mpart is None else jnp.maximum(mpart, blk)
        p = jnp.exp2((sc_ref[:, sl] - m_c).astype(BF16))
        vaug = jnp.concatenate([v_ref[0, sl, :], ones], axis=1)
        acc = acc + _dot(p, vaug)
    mp_ref[...] = jnp.broadcast_to(jnp.max(mpart, axis=1, keepdims=True), mp_ref.shape)
    ratio = acc[:, :LANES] / acc[:, LANES:]
    lam = (jnp.exp(jnp.sum(lam_ref[0:1, :] * lam_ref[1:2, :], axis=1, keepdims=True))
           - jnp.exp(jnp.sum(lam_ref[2:3, :] * lam_ref[3:4, :], axis=1, keepdims=True)) + LAM_INIT)
    o = ratio[:tq] - lam * ratio[tq:]
    var = jnp.mean(o * o, axis=-1, keepdims=True)
    o = o * lax.rsqrt(var + EPS) * g_ref[...] * (1.0 - LAM_INIT)
    o_ref[0] = o.astype(o_ref.dtype)


def _attn_kernel(q_ref, k_ref, v_ref, lam_ref, g_ref, o_ref, sa_ref, ma_ref, sb_ref, mb_ref, *, tk, nk):
    f = pl.program_id(0)
    body = functools.partial(_attn_body, q_ref, k_ref, v_ref, lam_ref, g_ref, o_ref, tk=tk, nk=nk)

    @pl.when(f == 0)
    def _():
        sb_ref[...] = jnp.zeros(sb_ref.shape, F32)
        mb_ref[...] = jnp.zeros(mb_ref.shape, F32)

    @pl.when(lax.rem(f, 2) == 0)
    def _():
        body(sa_ref, ma_ref, sb_ref, mb_ref)

    @pl.when(lax.rem(f, 2) == 1)
    def _():
        body(sb_ref, mb_ref, sa_ref, ma_ref)


def _diff_attention(qm, kh, proj3, v_col0, lam_rows, subln_g):
    b, t, _ = qm.shape
    r = kh.shape[1]
    tq = ROW_TILE
    tk = ROW_TILE
    n_q = t // tq
    total = b * DA_HEADS * n_q

    def split(idx):
        return idx // (DA_HEADS * n_q), lax.rem(idx // n_q, DA_HEADS), lax.rem(idx, n_q)

    prod = lambda f: split(jnp.minimum(f, total - 1))
    cons = lambda f: split(jnp.maximum(f - 1, 0))
    return pl.pallas_call(
        functools.partial(_attn_kernel, tk=tk, nk=r // tk),
        out_shape=jax.ShapeDtypeStruct((b, t, DA_HEADS * DA_DV), BF16),
        grid=(total + 1,),
        in_specs=[
            pl.BlockSpec((1, tq, 2 * LANES), lambda f: (prod(f)[0], prod(f)[2], prod(f)[1])),
            pl.BlockSpec((1, r, LANES), lambda f: (prod(f)[0], 0, prod(f)[1])),
            pl.BlockSpec((1, r, LANES), lambda f: (cons(f)[0], 0, v_col0 + cons(f)[1])),
            pl.BlockSpec((8, LANES), lambda f: (0, 0)),
            pl.BlockSpec((1, LANES), lambda f: (0, 0)),
        ],
        out_specs=pl.BlockSpec((1, tq, DA_DV), lambda f: (cons(f)[0], cons(f)[2], cons(f)[1])),
        scratch_shapes=[pltpu.VMEM((2 * tq, r), F32), pltpu.VMEM((2 * tq, LANES), F32),
                        pltpu.VMEM((2 * tq, r), F32), pltpu.VMEM((2 * tq, LANES), F32)],
        compiler_params=_cparams(("arbitrary",)),
        name="diff_attention",
    )(qm, kh, proj3, lam_rows, subln_g.reshape(1, DA_DV).astype(F32))


def _log_sigmoid(z):
    return jnp.minimum(z, 0.0) - jnp.log(1.0 + jnp.exp(-jnp.abs(z)))


def _gla_kernel(*refs, reverse, final, n_chunks):
    if final:
        (q_ref, k_ref, v_ref, lr_ref, w2_ref, b_ref, tri_ref, ones_ref,
         of_ref, gr_ref, ng_ref, o_ref, st_ref) = refs
    else:
        (q_ref, k_ref, v_ref, lr_ref, w2_ref, b_ref, tri_ref, ones_ref, o_ref, st_ref) = refs
    c_len = GLA_CHUNK

    @pl.when(pl.program_id(2) == 0)
    def _():
        st_ref[...] = jnp.zeros(st_ref.shape, F32)

    q = q_ref[0].astype(F32) * (GLA_DK ** -0.5)
    k = k_ref[0].astype(F32)
    v = v_ref[0]
    z = _dot3(lr_ref[0], w2_ref[0]) + b_ref[0]
    la = _log_sigmoid(z) * (1.0 / GLA_TAU)
    tri = tri_ref[...]
    cum = _exact_lhs_dot(tri, la)
    tot = _exact_lhs_dot(ones_ref[...], la)
    q_in = (q * jnp.exp(cum)).astype(BF16)
    k_in = (k * jnp.exp(-cum)).astype(BF16)
    k_st = (k * jnp.exp(tot - cum)).astype(BF16)
    dec = jnp.exp(tot)
    att = jnp.where(tri > 0, _dot_nt(q_in, k_in), 0.0)
    o_intra = _dot(att.astype(BF16), v)

    order = range(n_chunks - 1, -1, -1) if reverse else range(n_chunks)
    outs = [None] * n_chunks
    for c in order:
        sl = slice(c * c_len, (c + 1) * c_len)
        st = st_ref[...]
        outs[c] = o_intra[sl] + _dot_nt(q_in[sl], st.astype(BF16))
        kv = _dot_tn(v[sl], k_st[sl])
        st_ref[...] = st * dec[c * c_len:c * c_len + 1, :] + kv
    o = jnp.concatenate(outs, axis=0)
    if final:
        o = o + of_ref[0].astype(F32)
        var = jnp.mean(o * o, axis=-1, keepdims=True)
        y = o * lax.rsqrt(var + EPS) * ng_ref[...]
        gr = gr_ref[0].astype(F32)
        y = y * (gr * (1.0 / (1.0 + jnp.exp(-gr))))
        o_ref[0] = y.astype(o_ref.dtype)
    else:
        o_ref[0] = o.astype(o_ref.dtype)


def _gla_direction(proj3, lr3, w2p, bias, t_ctx, cols, reverse, o_fwd=None, norm_g=None):
    b, r, _ = proj3.shape
    tc = ROW_TILE
    n_chunks = tc // GLA_CHUNK
    ng = r // tc
    ncg = t_ctx // tc
    final = o_fwd is not None
    q_col, k_col, v_col, gr_col = cols

    blk = np.kron(np.eye(n_chunks), np.ones((GLA_CHUNK, GLA_CHUNK)))
    full_tri = np.tril(np.ones((tc, tc)))
    tri = blk * (full_tri.T if reverse else full_tri)
    tri = jnp.asarray(tri, BF16)
    ones_bd = jnp.asarray(blk, BF16)

    if reverse:
        def grp(s):
            return jnp.where(s < ncg, ncg - 1 - s, ng - 1 - (s - ncg))
    else:
        def grp(s):
            return s

    def lat(s):
        return jnp.maximum(grp(jnp.maximum(s, ncg)) - ncg, 0)

    in_specs = [
        pl.BlockSpec((1, tc, GLA_DK), lambda bi, h, s: (bi, grp(s), q_col + h)),
        pl.BlockSpec((1, tc, GLA_DK), lambda bi, h, s: (bi, grp(s), k_col + h)),
        pl.BlockSpec((1, tc, GLA_DV), lambda bi, h, s: (bi, grp(s), v_col + h)),
        pl.BlockSpec((1, tc, LANES), lambda bi, h, s: (bi, grp(s), 0)),
        pl.BlockSpec((1, LANES, GLA_DK), lambda bi, h, s: (h, 0, 0)),
        pl.BlockSpec((1, 1, GLA_DK), lambda bi, h, s: (h, 0, 0)),
        pl.BlockSpec((tc, tc), lambda bi, h, s: (0, 0)),
        pl.BlockSpec((tc, tc), lambda bi, h, s: (0, 0)),
    ]
    args = [proj3, proj3, proj3, lr3, w2p, bias, tri, ones_bd]
    if final:
        in_specs += [
            pl.BlockSpec((1, tc, GLA_DV), lambda bi, h, s: (bi, lat(s), h)),
            pl.BlockSpec((1, tc, GLA_DV), lambda bi, h, s: (bi, grp(s), gr_col + h)),
            pl.BlockSpec((1, GLA_DV), lambda bi, h, s: (0, 0)),
        ]
        args += [o_fwd, proj3, norm_g.reshape(1, GLA_DV).astype(F32)]
    return pl.pallas_call(
        functools.partial(_gla_kernel, reverse=reverse, final=final, n_chunks=n_chunks),
        out_shape=jax.ShapeDtypeStruct((b, r - t_ctx, GLA_HEADS * GLA_DV), BF16 if final else F32),
        grid=(b, GLA_HEADS, ng),
        in_specs=in_specs,
        out_specs=pl.BlockSpec((1, tc, GLA_DV), lambda bi, h, s: (bi, lat(s), h)),
        scratch_shapes=[pltpu.VMEM((GLA_DV, GLA_DK), F32)],
        compiler_params=_cparams(("arbitrary", "arbitrary", "arbitrary")),
        name="gla_bwd" if reverse else "gla_fwd",
    )(*args)


def _sigmoid(x):
    return 1.0 / (1.0 + jnp.exp(-x))


def _merge_kernel(ya_ref, yb_ref, wa_ref, wb_ref, ga_ref, gb_ref, o_ref):
    a = _dot(ya_ref[0], wa_ref[...])
    bb = _dot(yb_ref[0], wb_ref[...])
    z = _sigmoid(ga_ref[0].astype(F32)) * a + _sigmoid(gb_ref[0].astype(F32)) * bb
    o_ref[0] = z.astype(o_ref.dtype)


def _merge(y_a, y_b, wa, wb, proj3, ga_col0, gb_col0, t_ctx):
    b, t, ka = y_a.shape
    kb = y_b.shape[2]
    d = wa.shape[1]
    tm, tn = ROW_TILE, 1024
    row0 = t_ctx // tm
    return pl.pallas_call(
        _merge_kernel,
        out_shape=jax.ShapeDtypeStruct((b, t, d), BF16),
        grid=(d // tn, b, t // tm),
        in_specs=[
            pl.BlockSpec((1, tm, ka), lambda j, bi, i: (bi, i, 0)),
            pl.BlockSpec((1, tm, kb), lambda j, bi, i: (bi, i, 0)),
            pl.BlockSpec((ka, tn), lambda j, bi, i: (0, j)),
            pl.BlockSpec((kb, tn), lambda j, bi, i: (0, j)),
            pl.BlockSpec((1, tm, tn), lambda j, bi, i: (bi, row0 + i, ga_col0 // tn + j)),
            pl.BlockSpec((1, tm, tn), lambda j, bi, i: (bi, row0 + i, gb_col0 // tn + j)),
        ],
        out_specs=pl.BlockSpec((1, tm, tn), lambda j, bi, i: (bi, i, j)),
        compiler_params=_cparams(("arbitrary", "arbitrary", "arbitrary")),
        name="merge",
    )(y_a, y_b, wa, wb, proj3, proj3)


def _out_proj_kernel(z_ref, w_ref, x_ref, g_ref, o_ref):
    y = _dot(z_ref[0], w_ref[...])
    o_ref[0] = x_ref[0] + g_ref[0] * y


def _out_proj_residual(z, w, x, gate):
    b, t, d = x.shape
    tm, tn = 512, 1024
    return pl.pallas_call(
        _out_proj_kernel,
        out_shape=jax.ShapeDtypeStruct((b, t, d), F32),
        grid=(d // tn, b, t // tm),
        in_specs=[
            pl.BlockSpec((1, tm, d), lambda j, bi, i: (bi, i, 0)),
            pl.BlockSpec((d, tn), lambda j, bi, i: (0, j)),
            pl.BlockSpec((1, tm, tn), lambda j, bi, i: (bi, i, j)),
            pl.BlockSpec((1, 1, tn), lambda j, bi, i: (bi, 0, j)),
        ],
        out_specs=pl.BlockSpec((1, tm, tn), lambda j, bi, i: (bi, i, j)),
        compiler_params=_cparams(("arbitrary", "arbitrary", "arbitrary")),
        name="out_proj",
    )(z, w, x, gate)


def _peer_q_kernel(x_ref, ss_ref, g_ref, w_ref, h_ref, q_ref):
    h = _rms_mod(x_ref[0], g_ref[...], ss_ref[0, 0:1, :], ss_ref[0, 1:2, :]).astype(BF16)
    h_ref[0] = h
    q_ref[0] = _dot(h, w_ref[...])


def _peer_query(x1, ss, g, wq):
    b, t, d = x1.shape
    n = wq.shape[1]
    tm = ROW_TILE
    return pl.pallas_call(
        _peer_q_kernel,
        out_shape=(jax.ShapeDtypeStruct((b, t, d), BF16), jax.ShapeDtypeStruct((b, t, n), F32)),
        grid=(b, t // tm),
        in_specs=[
            pl.BlockSpec((1, tm, d), lambda bi, i: (bi, i, 0)),
            pl.BlockSpec((1, 2, d), lambda bi, i: (bi, 0, 0)),
            pl.BlockSpec((1, d), lambda bi, i: (0, 0)),
            pl.BlockSpec((d, n), lambda bi, i: (0, 0)),
        ],
        out_specs=(pl.BlockSpec((1, tm, d), lambda bi, i: (bi, i, 0)),
                   pl.BlockSpec((1, tm, n), lambda bi, i: (bi, i, 0))),
        compiler_params=_cparams(("arbitrary", "arbitrary")),
        name="peer_query",
    )(x1, ss, g, wq)


SUBLANES = 8


def _cmp_exchange(xs, hi, lo):
    a, b = xs[hi], xs[lo]
    if a is None and b is None:
        return
    if a is None or b is None:
        xs[hi], xs[lo] = (b if a is None else a), None
        return
    xs[hi], xs[lo] = jnp.maximum(a, b), jnp.minimum(a, b)


def _bitonic_sort_desc(xs):
    n = len(xs)
    xs = list(xs)
    k = 2
    while k <= n:
        j = k // 2
        while j >= 1:
            for i in range(n):
                l = i ^ j
                if l > i:
                    if (i & k) == 0:
                        _cmp_exchange(xs, i, l)
                    else:
                        _cmp_exchange(xs, l, i)
            j //= 2
        k *= 2
    return xs


def _bitonic_merge_desc(xs):
    n = len(xs)
    xs = list(xs)
    j = n // 2
    while j >= 1:
        for i in range(n):
            l = i ^ j
            if l > i:
                _cmp_exchange(xs, i, l)
        j //= 2
    return xs


def _top16_sorted(xs):
    n = PEER_TOPK
    xs = list(xs) + [None] * (n - len(xs))
    xs = _bitonic_sort_desc(xs)
    shift = SUBLANES // 2
    while shift >= 1:
        merged = []
        for i in range(n):
            a, b = xs[i], xs[n - 1 - i]
            if b is not None:
                b = pltpu.roll(b, shift, axis=0)
            merged.append(b if a is None else (a if b is None else jnp.maximum(a, b)))
        xs = _bitonic_merge_desc(merged)
        shift //= 2
    return xs


def _route_kernel(q_ref, keys_ref, s1_ref, s2_ref, e1_ref, e2_ref, tau_ref):
    q = q_ref[0]
    nk = PEER_NKEYS
    s1 = _dot3_nt(keys_ref[0, 0], q[:, :nk])
    s2 = _dot3_nt(keys_ref[0, 1], q[:, nk:])
    split = lambda s: [s[i * SUBLANES:(i + 1) * SUBLANES] for i in range(nk // SUBLANES)]
    v1 = _top16_sorted(split(s1))
    v2 = _top16_sorted(split(s2))
    sub = lax.broadcasted_iota(jnp.int32, v1[0].shape, 0)

    def spread(vals):
        out = vals[0]
        for j in range(1, SUBLANES):
            out = jnp.where(sub == j, vals[j], out)
        return out

    v2_lo, v2_hi, v1_hi = spread(v2[:SUBLANES]), spread(v2[SUBLANES:]), spread(v1[SUBLANES:])
    cands = [v1[0] + v2_lo, v1[0] + v2_hi]
    cands += [v1[i] + v2_lo for i in range(1, SUBLANES)]
    cands += [v1_hi + v2[0]]
    top = _top16_sorted(cands)
    m = top[0]
    zsum = jnp.zeros_like(m)
    for c in top:
        zsum = zsum + jnp.exp(c - m)
    s1_ref[0, 0] = s1
    s2_ref[0, 0] = s2
    e1_ref[0, 0] = jnp.exp(s1 - v1[0][0:1]) / zsum[0:1]
    e2_ref[0, 0] = jnp.exp(s2 - v2[0][0:1])
    tau_ref[0, 0] = top[PEER_TOPK - 1][0:1]


def _peer_route(q, keys):
    b, t, _ = q.shape
    tt = 512
    nk = PEER_NKEYS
    big = jax.ShapeDtypeStruct((b, PEER_HEADS, nk, t), F32)
    spec = pl.BlockSpec((1, 1, nk, tt), lambda bi, i, h: (bi, h, 0, i))
    return pl.pallas_call(
        _route_kernel,
        out_shape=(big, big, big, big, jax.ShapeDtypeStruct((b, PEER_HEADS, 1, t), F32)),
        grid=(b, t // tt, PEER_HEADS),
        in_specs=[
            pl.BlockSpec((1, tt, 2 * nk), lambda bi, i, h: (bi, i, h)),
            pl.BlockSpec((1, 2, nk, nk), lambda bi, i, h: (h, 0, 0, 0)),
        ],
        out_specs=(spec, spec, spec, spec,
                   pl.BlockSpec((1, 1, 1, tt), lambda bi, i, h: (bi, h, 0, i))),
        compiler_params=_cparams(("arbitrary", "arbitrary", "arbitrary")),
        name="peer_route",
    )(q, keys)


def _gelu(x):
    return 0.5 * x * (1.0 + lax.erf(x * (2.0 ** -0.5)))


def _peer_mix_kernel(h_ref, u_ref, vt_ref, s1_ref, s2_ref, e1_ref, e2_ref, tau_ref,
                     o_ref, st_ref, gt_ref, *, n_a, n_e):
    f = pl.program_id(0)
    nk = PEER_NKEYS
    slot_p = lax.rem(f, 2)
    slot_c = 1 - slot_p
    e_c = lax.rem(jnp.maximum(f - 1, 0), n_e)

    @pl.when(f == 0)
    def _():
        st_ref[1] = jnp.zeros(st_ref.shape[1:], F32)
        gt_ref[1] = jnp.zeros(gt_ref.shape[1:], F32)

    @pl.when(e_c == 0)
    def _():
        o_ref[...] = jnp.zeros(o_ref.shape, F32)

    a_t = (_gelu(st_ref[slot_c]) * gt_ref[slot_c]).astype(BF16)
    o_ref[0] += _dot(vt_ref[...], a_t)

    st_ref[slot_p] = _dot_nt(u_ref[...], h_ref[0])
    for al in range(n_a):
        gate = None
        for h in range(PEER_HEADS):
            cand = s1_ref[0, h, al:al + 1, :] + s2_ref[0, h]
            w = e1_ref[0, h, al:al + 1, :] * e2_ref[0, h]
            term = jnp.where(cand >= tau_ref[0, h], w, 0.0)
            gate = term if gate is None else gate + term
        gt_ref[slot_p, al * nk:(al + 1) * nk, :] = gate


def _peer_mix(h2, u_bf, vt_bf, s1, s2, e1, e2, tau):
    b, t, d = h2.shape
    nk = PEER_NKEYS
    n_exp = u_bf.shape[0]
    tt = 512
    n_a = 8
    eb = n_a * nk
    n_e = n_exp // eb
    n_t = t // tt
    total = b * n_t * n_e

    def split(idx):
        return idx // (n_t * n_e), lax.rem(idx // n_e, n_t), lax.rem(idx, n_e)

    prod = lambda f: split(jnp.minimum(f, total - 1))
    cons = lambda f: split(jnp.maximum(f - 1, 0))

    def tok_spec(rows):
        def imap(f):
            bi, ti, _ = prod(f)
            return (bi, 0, 0, ti)
        return pl.BlockSpec((1, PEER_HEADS, rows, tt), imap)

    def blk_spec():
        def imap(f):
            bi, ti, e = prod(f)
            return (bi, 0, e, ti)
        return pl.BlockSpec((1, PEER_HEADS, n_a, tt), imap)

    return pl.pallas_call(
        functools.partial(_peer_mix_kernel, n_a=n_a, n_e=n_e),
        out_shape=jax.ShapeDtypeStruct((b, d, t), F32),
        grid=(total + 1,),
        in_specs=[
            pl.BlockSpec((1, tt, d), lambda f: (prod(f)[0], prod(f)[1], 0)),
            pl.BlockSpec((eb, d), lambda f: (prod(f)[2], 0)),
            pl.BlockSpec((d, eb), lambda f: (0, cons(f)[2])),
            blk_spec(), tok_spec(nk), blk_spec(), tok_spec(nk), tok_spec(1),
        ],
        out_specs=pl.BlockSpec((1, d, tt), lambda f: (cons(f)[0], 0, cons(f)[1])),
        scratch_shapes=[pltpu.VMEM((2, eb, tt), F32), pltpu.VMEM((2, eb, tt), F32)],
        compiler_params=_cparams(("arbitrary",)),
        name="peer_mix",
    )(h2, u_bf, vt_bf, s1, s2, e1, e2, tau)


def _peer_out_kernel(acc_ref, x_ref, g_ref, o_ref):
    o_ref[0] = x_ref[0] + g_ref[0] * acc_ref[0].T


def _peer_residual(acc_t, x1, gate):
    b, t, d = x1.shape
    tt = 512
    return pl.pallas_call(
        _peer_out_kernel,
        out_shape=jax.ShapeDtypeStruct((b, t, d), F32),
        grid=(b, t // tt),
        in_specs=[
            pl.BlockSpec((1, d, tt), lambda bi, i: (bi, 0, i)),
            pl.BlockSpec((1, tt, d), lambda bi, i: (bi, i, 0)),
            pl.BlockSpec((1, 1, d), lambda bi, i: (bi, 0, 0)),
        ],
        out_specs=pl.BlockSpec((1, tt, d), lambda bi, i: (bi, i, 0)),
        compiler_params=_cparams(("arbitrary", "arbitrary")),
        name="peer_residual",
    )(acc_t, x1, gate)


def kernel(x, c, ctx, c_ctx, ada_w, ada_b, norm1_g, norm2_g, w_in, da_qn_g, da_kn_g, da_lam_q1, da_lam_k1, da_lam_q2, da_lam_k2, da_subln_g, gla_w2_f, gla_b_f, gla_w2_b, gla_b_b, gla_norm_g, w_br_a, w_br_b, w_out, peer_wq, peer_keys, peer_u, peer_v):
    b, t, d = x.shape
    t_ctx = ctx.shape[1]
    assert ada_w.shape[0] == 1, "single-layer kernel"
    assert t % ROW_TILE == 0 and t_ctx % ROW_TILE == 0 and t % GRID_W == 0
    l = 0

    cond = jnp.concatenate([c, c_ctx[None, :]], axis=0)
    mod = _ada_modulation(cond, ada_w[l], ada_b[l])[:b + 1].reshape(b + 1, 6, d)
    ss1 = mod[:, 0:2]
    g1 = mod[:b, 2:3]
    ss2 = mod[:b, 3:5]
    g2 = mod[:b, 5:6]

    h_all = _norm_mod_all(x, ctx, ss1, norm1_g[l].reshape(1, d))
    r = t_ctx + t
    wi = w_in[l]
    o_glr = 6144
    w_main = jnp.concatenate([wi[:, :o_glr], wi[:, o_glr + 2 * GLA_RANK:]], axis=1).astype(BF16)
    w_lr = jnp.pad(wi[:, o_glr:o_glr + 2 * GLA_RANK], ((0, 0), (0, LANES - 2 * GLA_RANK))).astype(BF16)
    h2d = h_all.reshape(b * r, d)
    n_main = w_main.shape[1]
    proj3 = _matmul(h2d, w_main, 512, 1024, BF16, "in_proj").reshape(b, r, n_main)
    lr3 = _matmul(h2d, w_lr, 512, LANES, F32, "in_proj_lr").reshape(b, r, LANES)

    cos_t, sa_t, sb_t = _rope_tables(t, t_ctx)
    qm = _qk_prep(proj3, cos_t, sa_t, sb_t, da_qn_g[l], 0, True, t_ctx)
    kh = _qk_prep(proj3, cos_t, sa_t, sb_t, da_kn_g[l], 8, False, t_ctx)
    lam_rows = jnp.zeros((8, LANES), F32)
    lam_rows = lam_rows.at[0, :DA_DH].set(da_lam_q1[l]).at[1, :DA_DH].set(da_lam_k1[l])
    lam_rows = lam_rows.at[2, :DA_DH].set(da_lam_q2[l]).at[3, :DA_DH].set(da_lam_k2[l])
    y_a = _diff_attention(qm, kh, proj3, 16, lam_rows, da_subln_g[l])

    def pad_w2(w2, row0):
        w = w2.reshape(GLA_RANK, GLA_HEADS, GLA_DK).transpose(1, 0, 2)
        return jnp.pad(w, ((0, 0), (row0, LANES - GLA_RANK - row0), (0, 0))).astype(F32)

    cols = (24, 28, 16, 20)
    o_f = _gla_direction(proj3, lr3, pad_w2(gla_w2_f[l], 0),
                         gla_b_f[l].reshape(GLA_HEADS, 1, GLA_DK), t_ctx, cols, False)
    y_b = _gla_direction(proj3, lr3, pad_w2(gla_w2_b[l], GLA_RANK),
                         gla_b_b[l].reshape(GLA_HEADS, 1, GLA_DK), t_ctx, cols, True,
                         o_fwd=o_f, norm_g=gla_norm_g[l])

    z = _merge(y_a, y_b, w_br_a[l].astype(BF16), w_br_b[l].astype(BF16), proj3, 6144, 8192, t_ctx)
    x1 = _out_proj_residual(z, w_out[l].astype(BF16), x, g1)

    h2, q = _peer_query(x1, ss2, norm2_g[l].reshape(1, d), peer_wq[l].astype(BF16))
    s1, s2, e1, e2, tau = _peer_route(q, peer_keys[l])
    acc_t = _peer_mix(h2, peer_u[l].astype(BF16), peer_v[l].T.astype(BF16), s1, s2, e1, e2, tau)
    return _peer_residual(acc_t, x1, g2)
```

```python
import functools
import math

import jax
import jax.numpy as jnp
import numpy as np
from jax import lax
from jax.experimental import pallas as pl
from jax.experimental.pallas import tpu as pltpu

F32 = jnp.float32
BF16 = jnp.bfloat16

GRID_W = 64
DA_HEADS = 8
DA_DH = 64
DA_DV = 128
GLA_HEADS = 4
GLA_DK = 128
GLA_DV = 256
GLA_RANK = 16
GLA_TAU = 16.0
GLA_CHUNK = 64
PEER_HEADS = 8
PEER_NKEYS = 128
PEER_TOPK = 16
ROPE_BASE = 10000.0
EPS = 1e-6
LAM_INIT = 0.8 - 0.6 * math.exp(-0.3 * 0)

LANES = 128
ROW_TILE = 256
VMEM_LIMIT = 56 * 1024 * 1024

NEG_INF = float("-inf")
LOG2_E = 1.0 / math.log(2.0)


def _cparams(sem):
    return pltpu.CompilerParams(dimension_semantics=sem, vmem_limit_bytes=VMEM_LIMIT)


def _split2(x):
    hi = x.astype(BF16)
    lo = (x - hi.astype(F32)).astype(BF16)
    return hi, lo


def _split3(x):
    p1 = x.astype(BF16)
    r1 = x - p1.astype(F32)
    p2 = r1.astype(BF16)
    p3 = (r1 - p2.astype(F32)).astype(BF16)
    return p1, p2, p3


def _dot(a, b):
    return jnp.dot(a, b, preferred_element_type=F32)


def _dot_nt(a, b):
    return lax.dot_general(a, b, (((1,), (1,)), ((), ())), preferred_element_type=F32)


def _dot_tn(a, b):
    return lax.dot_general(a, b, (((0,), (0,)), ((), ())), preferred_element_type=F32)


def _dot3_nt(a, b):
    ah, al = _split2(a)
    bh, bl = _split2(b)
    return _dot_nt(ah, bh) + _dot_nt(ah, bl) + _dot_nt(al, bh)


def _dot3(a, b):
    ah, al = _split2(a)
    bh, bl = _split2(b)
    return _dot(ah, bh) + _dot(ah, bl) + _dot(al, bh)


def _exact_lhs_dot(m_bf16, x):
    p1, p2, p3 = _split3(x)
    return _dot(m_bf16, p1) + _dot(m_bf16, p2) + _dot(m_bf16, p3)


def _ada_kernel(cb_ref, w_ref, b_ref, o_ref, *, n_rows):
    tn = w_ref.shape[1]
    o_ref[...] = jnp.zeros(o_ref.shape, F32)
    for r in range(n_rows):
        cv = cb_ref[r]
        s = cv * (1.0 / (1.0 + jnp.exp(-cv)))
        for j in range(tn // LANES):
            sl = slice(j * LANES, (j + 1) * LANES)
            col = jnp.sum(s * w_ref[:, sl], axis=0, keepdims=True)
            o_ref[r:r + 1, sl] = col + b_ref[:, sl]


def _ada_modulation(cond, ada_w, ada_b):
    n_rows, d = cond.shape
    n = ada_w.shape[1]
    tn = 1024
    cb = jnp.broadcast_to(cond[:, :, None], (n_rows, d, LANES))
    return pl.pallas_call(
        functools.partial(_ada_kernel, n_rows=n_rows),
        out_shape=jax.ShapeDtypeStruct((8, n), F32),
        grid=(n // tn,),
        in_specs=[
            pl.BlockSpec((n_rows, d, LANES), lambda j: (0, 0, 0)),
            pl.BlockSpec((d, tn), lambda j: (0, j)),
            pl.BlockSpec((1, tn), lambda j: (0, j)),
        ],
        out_specs=pl.BlockSpec((8, tn), lambda j: (0, j)),
        compiler_params=_cparams(("arbitrary",)),
        name="ada_modulation",
    )(cb, ada_w, ada_b.reshape(1, n))


def _rms_mod(xf, g, shift, scale):
    var = jnp.mean(xf * xf, axis=-1, keepdims=True)
    y = xf * lax.rsqrt(var + EPS) * g
    return y * (1.0 + scale) + shift


def _norm_mod_kernel(x_ref, ctx_ref, ss_ref, g_ref, o_ref, *, nc):
    i = pl.program_id(1)

    def emit(src):
        h = _rms_mod(src[0], g_ref[...], ss_ref[0, 0:1, :], ss_ref[0, 1:2, :])
        o_ref[0] = h.astype(o_ref.dtype)

    @pl.when(i < nc)
    def _():
        emit(ctx_ref)

    @pl.when(i >= nc)
    def _():
        emit(x_ref)


def _norm_mod_all(x, ctx, ss, g):
    b, t, d = x.shape
    tc = ctx.shape[1]
    tm = ROW_TILE
    nc = tc // tm
    r = tc + t
    return pl.pallas_call(
        functools.partial(_norm_mod_kernel, nc=nc),
        out_shape=jax.ShapeDtypeStruct((b, r, d), BF16),
        grid=(b, r // tm),
        in_specs=[
            pl.BlockSpec((1, tm, d), lambda bi, i: (bi, jnp.maximum(i - nc, 0), 0)),
            pl.BlockSpec((1, tm, d), lambda bi, i: (bi, jnp.minimum(i, nc - 1), 0)),
            pl.BlockSpec((1, 2, d), lambda bi, i: (jnp.where(i < nc, b, bi), 0, 0)),
            pl.BlockSpec((1, d), lambda bi, i: (0, 0)),
        ],
        out_specs=pl.BlockSpec((1, tm, d), lambda bi, i: (bi, i, 0)),
        compiler_params=_cparams(("arbitrary", "arbitrary")),
        name="norm_mod",
    )(x, ctx, ss, g)


def _matmul_kernel(a_ref, b_ref, o_ref):
    o_ref[...] = _dot(a_ref[...], b_ref[...]).astype(o_ref.dtype)


def _matmul(a, b, tm, tn, out_dtype, name):
    m, k = a.shape
    n = b.shape[1]
    return pl.pallas_call(
        _matmul_kernel,
        out_shape=jax.ShapeDtypeStruct((m, n), out_dtype),
        grid=(n // tn, m // tm),
        in_specs=[
            pl.BlockSpec((tm, k), lambda j, i: (i, 0)),
            pl.BlockSpec((k, tn), lambda j, i: (0, j)),
        ],
        out_specs=pl.BlockSpec((tm, tn), lambda j, i: (i, j)),
        compiler_params=_cparams(("arbitrary", "arbitrary")),
        name=name,
    )(a, b)


def _rope_tables(t_lat, t_ctx):
    rows = t_lat // GRID_W
    row = np.repeat(np.arange(rows), GRID_W).astype(np.float64)
    col = np.tile(np.arange(GRID_W), rows).astype(np.float64)
    half = DA_DH // 2
    inv = ROPE_BASE ** (-np.arange(0, half, 2, dtype=np.float64) / half)
    ar = row[:, None] * inv
    ac = col[:, None] * inv
    cr, sr, cc, sc = np.cos(ar), np.sin(ar), np.cos(ac), np.sin(ac)
    z = np.zeros_like(sr)
    cos64 = np.concatenate([cr, cr, cc, cc], axis=1)
    sa64 = np.concatenate([-sr, z, -sc, z], axis=1)
    sb64 = np.concatenate([z, sr, z, sc], axis=1)
    tile2 = lambda a: np.concatenate([a, a], axis=1)
    pad = lambda a, v: jnp.asarray(
        np.concatenate([np.full((t_ctx, LANES), v), tile2(a)], axis=0), F32)
    return pad(cos64, 1.0), pad(sa64, 0.0), pad(sb64, 0.0)


def _qk_prep_kernel(x_ref, cos_ref, sa_ref, sb_ref, g_ref, bd_ref, o_ref, *, is_q):
    lane = lax.broadcasted_iota(jnp.int32, cos_ref.shape, 1)
    for h in range(DA_HEADS):
        x = x_ref[0, :, h * LANES:(h + 1) * LANES].astype(F32)
        ss = _exact_lhs_dot_rhs(x * x, bd_ref[...])
        y = x * lax.rsqrt(ss * (1.0 / DA_DH) + EPS) * g_ref[...]
        up = pltpu.roll(y, LANES - 16, axis=1)
        dn = pltpu.roll(y, 16, axis=1)
        o = y * cos_ref[...] + up * sa_ref[...] + dn * sb_ref[...]
        if is_q:
            o = o * (DA_DH ** -0.5 * LOG2_E)
            zero = jnp.zeros_like(o)
            o_ref[0, :, 2 * h * LANES:(2 * h + 1) * LANES] = jnp.where(lane < DA_DH, o, zero).astype(o_ref.dtype)
            o_ref[0, :, (2 * h + 1) * LANES:(2 * h + 2) * LANES] = jnp.where(lane >= DA_DH, o, zero).astype(o_ref.dtype)
        else:
            o_ref[0, :, h * LANES:(h + 1) * LANES] = o.astype(o_ref.dtype)


def _exact_lhs_dot_rhs(x, m_bf16):
    p1, p2, p3 = _split3(x)
    return _dot(p1, m_bf16) + _dot(p2, m_bf16) + _dot(p3, m_bf16)


def _qk_prep(proj3, cos_t, sa_t, sb_t, gain, col0, is_q, t_ctx):
    b, r, _ = proj3.shape
    tm = ROW_TILE
    row0 = t_ctx // tm if is_q else 0
    n_rows = r - t_ctx if is_q else r
    width = DA_HEADS * (2 * LANES if is_q else LANES)
    bd = jnp.asarray(np.kron(np.eye(2), np.ones((DA_DH, DA_DH))), BF16)
    g2 = jnp.concatenate([gain, gain]).reshape(1, LANES).astype(F32)
    return pl.pallas_call(
        functools.partial(_qk_prep_kernel, is_q=is_q),
        out_shape=jax.ShapeDtypeStruct((b, n_rows, width), BF16),
        grid=(b, n_rows // tm),
        in_specs=[
            pl.BlockSpec((1, tm, DA_HEADS * LANES), lambda bi, i: (bi, row0 + i, col0 // DA_HEADS)),
            pl.BlockSpec((tm, LANES), lambda bi, i: (row0 + i, 0)),
            pl.BlockSpec((tm, LANES), lambda bi, i: (row0 + i, 0)),
            pl.BlockSpec((tm, LANES), lambda bi, i: (row0 + i, 0)),
            pl.BlockSpec((1, LANES), lambda bi, i: (0, 0)),
            pl.BlockSpec((LANES, LANES), lambda bi, i: (0, 0)),
        ],
        out_specs=pl.BlockSpec((1, tm, width), lambda bi, i: (bi, i, 0)),
        compiler_params=_cparams(("arbitrary", "arbitrary")),
        name="q_prep" if is_q else "k_prep",
    )(proj3, cos_t, sa_t, sb_t, g2, bd)


def _attn_body(q_ref, k_ref, v_ref, lam_ref, g_ref, o_ref, sp_ref, mp_ref, sc_ref, mc_ref, *, tk, nk):
    tq = q_ref.shape[1]
    q = q_ref[0]
    qs = jnp.concatenate([q[:, :LANES], q[:, LANES:]], axis=0)
    m_c = mc_ref[:, 0:1]
    ones = jnp.ones((tk, LANES), BF16)
    acc = jnp.zeros((2 * tq, 2 * LANES), F32)
    mpart = None
    for c in range(nk):
        sl = slice(c * tk, (c + 1) * tk)
        s = _dot_nt(qs, k_ref[0, sl, :])
        sp_ref[:, sl] = s
        for j in range(tk // LANES):
            blk = s[:, j * LANES:(j + 1) * LANES]
            mpart = blk if mpart is None else jnp.maximum(mpart, blk)
        p = jnp.exp2((sc_ref[:, sl] - m_c).astype(BF16))
        vaug = jnp.concatenate([v_ref[0, sl, :], ones], axis=1)
        acc = acc + _dot(p, vaug)
    mp_ref[...] = jnp.broadcast_to(jnp.max(mpart, axis=1, keepdims=True), mp_ref.shape)
    ratio = acc[:, :LANES] / acc[:, LANES:]
    lam = (jnp.exp(jnp.sum(lam_ref[0:1, :] * lam_ref[1:2, :], axis=1, keepdims=True))
           - jnp.exp(jnp.sum(lam_ref[2:3, :] * lam_ref[3:4, :], axis=1, keepdims=True)) + LAM_INIT)
    o = ratio[:tq] - lam * ratio[tq:]
    var = jnp.mean(o * o, axis=-1, keepdims=True)
    o = o * lax.rsqrt(var + EPS) * g_ref[...] * (1.0 - LAM_INIT)
    o_ref[0] = o.astype(o_ref.dtype)


def _attn_kernel(q_ref, k_ref, v_ref, lam_ref, g_ref, o_ref, sa_ref, ma_ref, sb_ref, mb_ref, *, tk, nk):
    f = pl.program_id(0)
    body = functools.partial(_attn_body, q_ref, k_ref, v_ref, lam_ref, g_ref, o_ref, tk=tk, nk=nk)

    @pl.when(f == 0)
    def _():
        sb_ref[...] = jnp.zeros(sb_ref.shape, F32)
        mb_ref[...] = jnp.zeros(mb_ref.shape, F32)

    @pl.when(lax.rem(f, 2) == 0)
    def _():
        body(sa_ref, ma_ref, sb_ref, mb_ref)

    @pl.when(lax.rem(f, 2) == 1)
    def _():
        body(sb_ref, mb_ref, sa_ref, ma_ref)


def _diff_attention(qm, kh, proj3, v_col0, lam_rows, subln_g):
    b, t, _ = qm.shape
    r = kh.shape[1]
    tq = ROW_TILE
    tk = ROW_TILE
    n_q = t // tq
    total = b * DA_HEADS * n_q

    def split(idx):
        return idx // (DA_HEADS * n_q), lax.rem(idx // n_q, DA_HEADS), lax.rem(idx, n_q)

    prod = lambda f: split(jnp.minimum(f, total - 1))
    cons = lambda f: split(jnp.maximum(f - 1, 0))
    return pl.pallas_call(
        functools.partial(_attn_kernel, tk=tk, nk=r // tk),
        out_shape=jax.ShapeDtypeStruct((b, t, DA_HEADS * DA_DV), BF16),
        grid=(total + 1,),
        in_specs=[
            pl.BlockSpec((1, tq, 2 * LANES), lambda f: (prod(f)[0], prod(f)[2], prod(f)[1])),
            pl.BlockSpec((1, r, LANES), lambda f: (prod(f)[0], 0, prod(f)[1])),
            pl.BlockSpec((1, r, LANES), lambda f: (cons(f)[0], 0, v_col0 + cons(f)[1])),
            pl.BlockSpec((8, LANES), lambda f: (0, 0)),
            pl.BlockSpec((1, LANES), lambda f: (0, 0)),
        ],
        out_specs=pl.BlockSpec((1, tq, DA_DV), lambda f: (cons(f)[0], cons(f)[2], cons(f)[1])),
        scratch_shapes=[pltpu.VMEM((2 * tq, r), F32), pltpu.VMEM((2 * tq, LANES), F32),
                        pltpu.VMEM((2 * tq, r), F32), pltpu.VMEM((2 * tq, LANES), F32)],
        compiler_params=_cparams(("arbitrary",)),
        name="diff_attention",
    )(qm, kh, proj3, lam_rows, subln_g.reshape(1, DA_DV).astype(F32))


def _log_sigmoid(z):
    return jnp.minimum(z, 0.0) - jnp.log(1.0 + jnp.exp(-jnp.abs(z)))


def _gla_kernel(*refs, reverse, final, n_chunks):
    if final:
        (q_ref, k_ref, v_ref, lr_ref, w2_ref, b_ref, tri_ref, ones_ref,
         of_ref, gr_ref, ng_ref, o_ref, st_ref) = refs
    else:
        (q_ref, k_ref, v_ref, lr_ref, w2_ref, b_ref, tri_ref, ones_ref, o_ref, st_ref) = refs
    c_len = GLA_CHUNK

    @pl.when(pl.program_id(2) == 0)
    def _():
        st_ref[...] = jnp.zeros(st_ref.shape, F32)

    q = q_ref[0].astype(F32) * (GLA_DK ** -0.5)
    k = k_ref[0].astype(F32)
    v = v_ref[0]
    z = _dot3(lr_ref[0], w2_ref[0]) + b_ref[0]
    la = _log_sigmoid(z) * (1.0 / GLA_TAU)
    tri = tri_ref[...]
    cum = _exact_lhs_dot(tri, la)
    tot = _exact_lhs_dot(ones_ref[...], la)
    q_in = (q * jnp.exp(cum)).astype(BF16)
    k_in = (k * jnp.exp(-cum)).astype(BF16)
    k_st = (k * jnp.exp(tot - cum)).astype(BF16)
    dec = jnp.exp(tot)
    att = jnp.where(tri > 0, _dot_nt(q_in, k_in), 0.0)
    o_intra = _dot(att.astype(BF16), v)

    order = range(n_chunks - 1, -1, -1) if reverse else range(n_chunks)
    outs = [None] * n_chunks
    for c in order:
        sl = slice(c * c_len, (c + 1) * c_len)
        st = st_ref[...]
        outs[c] = o_intra[sl] + _dot_nt(q_in[sl], st.astype(BF16))
        kv = _dot_tn(v[sl], k_st[sl])
        st_ref[...] = st * dec[c * c_len:c * c_len + 1, :] + kv
    o = jnp.concatenate(outs, axis=0)
    if final:
        o = o + of_ref[0].astype(F32)
        var = jnp.mean(o * o, axis=-1, keepdims=True)
        y = o * lax.rsqrt(var + EPS) * ng_ref[...]
        gr = gr_ref[0].astype(F32)
        y = y * (gr * (1.0 / (1.0 + jnp.exp(-gr))))
        o_ref[0] = y.astype(o_ref.dtype)
    else:
        o_ref[0] = o.astype(o_ref.dtype)


def _gla_direction(proj3, lr3, w2p, bias, t_ctx, cols, reverse, o_fwd=None, norm_g=None):
    b, r, _ = proj3.shape
    tc = ROW_TILE
    n_chunks = tc // GLA_CHUNK
    ng = r // tc
    ncg = t_ctx // tc
    final = o_fwd is not None
    q_col, k_col, v_col, gr_col = cols

    blk = np.kron(np.eye(n_chunks), np.ones((GLA_CHUNK, GLA_CHUNK)))
    full_tri = np.tril(np.ones((tc, tc)))
    tri = blk * (full_tri.T if reverse else full_tri)
    tri = jnp.asarray(tri, BF16)
    ones_bd = jnp.asarray(blk, BF16)

    if reverse:
        def grp(s):
            return jnp.where(s < ncg, ncg - 1 - s, ng - 1 - (s - ncg))
    else:
        def grp(s):
            return s

    def lat(s):
        return jnp.maximum(grp(jnp.maximum(s, ncg)) - ncg, 0)

    in_specs = [
        pl.BlockSpec((1, tc, GLA_DK), lambda bi, h, s: (bi, grp(s), q_col + h)),
        pl.BlockSpec((1, tc, GLA_DK), lambda bi, h, s: (bi, grp(s), k_col + h)),
        pl.BlockSpec((1, tc, GLA_DV), lambda bi, h, s: (bi, grp(s), v_col + h)),
        pl.BlockSpec((1, tc, LANES), lambda bi, h, s: (bi, grp(s), 0)),
        pl.BlockSpec((1, LANES, GLA_DK), lambda bi, h, s: (h, 0, 0)),
        pl.BlockSpec((1, 1, GLA_DK), lambda bi, h, s: (h, 0, 0)),
        pl.BlockSpec((tc, tc), lambda bi, h, s: (0, 0)),
        pl.BlockSpec((tc, tc), lambda bi, h, s: (0, 0)),
    ]
    args = [proj3, proj3, proj3, lr3, w2p, bias, tri, ones_bd]
    if final:
        in_specs += [
            pl.BlockSpec((1, tc, GLA_DV), lambda bi, h, s: (bi, lat(s), h)),
            pl.BlockSpec((1, tc, GLA_DV), lambda bi, h, s: (bi, grp(s), gr_col + h)),
            pl.BlockSpec((1, GLA_DV), lambda bi, h, s: (0, 0)),
        ]
        args += [o_fwd, proj3, norm_g.reshape(1, GLA_DV).astype(F32)]
    return pl.pallas_call(
        functools.partial(_gla_kernel, reverse=reverse, final=final, n_chunks=n_chunks),
        out_shape=jax.ShapeDtypeStruct((b, r - t_ctx, GLA_HEADS * GLA_DV), BF16 if final else F32),
        grid=(b, GLA_HEADS, ng),
        in_specs=in_specs,
        out_specs=pl.BlockSpec((1, tc, GLA_DV), lambda bi, h, s: (bi, lat(s), h)),
        scratch_shapes=[pltpu.VMEM((GLA_DV, GLA_DK), F32)],
        compiler_params=_cparams(("arbitrary", "arbitrary", "arbitrary")),
        name="gla_bwd" if reverse else "gla_fwd",
    )(*args)


def _sigmoid(x):
    return 1.0 / (1.0 + jnp.exp(-x))


def _merge_kernel(ya_ref, yb_ref, wa_ref, wb_ref, ga_ref, gb_ref, o_ref):
    a = _dot(ya_ref[0], wa_ref[...])
    bb = _dot(yb_ref[0], wb_ref[...])
    z = _sigmoid(ga_ref[0].astype(F32)) * a + _sigmoid(gb_ref[0].astype(F32)) * bb
    o_ref[0] = z.astype(o_ref.dtype)


def _merge(y_a, y_b, wa, wb, proj3, ga_col0, gb_col0, t_ctx):
    b, t, ka = y_a.shape
    kb = y_b.shape[2]
    d = wa.shape[1]
    tm, tn = ROW_TILE, 1024
    row0 = t_ctx // tm
    return pl.pallas_call(
        _merge_kernel,
        out_shape=jax.ShapeDtypeStruct((b, t, d), BF16),
        grid=(d // tn, b, t // tm),
        in_specs=[
            pl.BlockSpec((1, tm, ka), lambda j, bi, i: (bi, i, 0)),
            pl.BlockSpec((1, tm, kb), lambda j, bi, i: (bi, i, 0)),
            pl.BlockSpec((ka, tn), lambda j, bi, i: (0, j)),
            pl.BlockSpec((kb, tn), lambda j, bi, i: (0, j)),
            pl.BlockSpec((1, tm, tn), lambda j, bi, i: (bi, row0 + i, ga_col0 // tn + j)),
            pl.BlockSpec((1, tm, tn), lambda j, bi, i: (bi, row0 + i, gb_col0 // tn + j)),
        ],
        out_specs=pl.BlockSpec((1, tm, tn), lambda j, bi, i: (bi, i, j)),
        compiler_params=_cparams(("arbitrary", "arbitrary", "arbitrary")),
        name="merge",
    )(y_a, y_b, wa, wb, proj3, proj3)


def _out_proj_kernel(z_ref, w_ref, x_ref, g_ref, o_ref):
    y = _dot(z_ref[0], w_ref[...])
    o_ref[0] = x_ref[0] + g_ref[0] * y


def _out_proj_residual(z, w, x, gate):
    b, t, d = x.shape
    tm, tn = 512, 1024
    return pl.pallas_call(
        _out_proj_kernel,
        out_shape=jax.ShapeDtypeStruct((b, t, d), F32),
        grid=(d // tn, b, t // tm),
        in_specs=[
            pl.BlockSpec((1, tm, d), lambda j, bi, i: (bi, i, 0)),
            pl.BlockSpec((d, tn), lambda j, bi, i: (0, j)),
            pl.BlockSpec((1, tm, tn), lambda j, bi, i: (bi, i, j)),
            pl.BlockSpec((1, 1, tn), lambda j, bi, i: (bi, 0, j)),
        ],
        out_specs=pl.BlockSpec((1, tm, tn), lambda j, bi, i: (bi, i, j)),
        compiler_params=_cparams(("arbitrary", "arbitrary", "arbitrary")),
        name="out_proj",
    )(z, w, x, gate)


def _peer_q_kernel(x_ref, ss_ref, g_ref, w_ref, h_ref, q_ref):
    h = _rms_mod(x_ref[0], g_ref[...], ss_ref[0, 0:1, :], ss_ref[0, 1:2, :]).astype(BF16)
    h_ref[0] = h
    q_ref[0] = _dot(h, w_ref[...])


def _peer_query(x1, ss, g, wq):
    b, t, d = x1.shape
    n = wq.shape[1]
    tm = ROW_TILE
    return pl.pallas_call(
        _peer_q_kernel,
        out_shape=(jax.ShapeDtypeStruct((b, t, d), BF16), jax.ShapeDtypeStruct((b, t, n), F32)),
        grid=(b, t // tm),
        in_specs=[
            pl.BlockSpec((1, tm, d), lambda bi, i: (bi, i, 0)),
            pl.BlockSpec((1, 2, d), lambda bi, i: (bi, 0, 0)),
            pl.BlockSpec((1, d), lambda bi, i: (0, 0)),
            pl.BlockSpec((d, n), lambda bi, i: (0, 0)),
        ],
        out_specs=(pl.BlockSpec((1, tm, d), lambda bi, i: (bi, i, 0)),
                   pl.BlockSpec((1, tm, n), lambda bi, i: (bi, i, 0))),
        compiler_params=_cparams(("arbitrary", "arbitrary")),
        name="peer_query",
    )(x1, ss, g, wq)


SUBLANES = 8


def _cmp_exchange(xs, hi, lo):
    a, b = xs[hi], xs[lo]
    if a is None and b is None:
        return
    if a is None or b is None:
        xs[hi], xs[lo] = (b if a is None else a), None
        return
    xs[hi], xs[lo] = jnp.maximum(a, b), jnp.minimum(a, b)


def _bitonic_sort_desc(xs):
    n = len(xs)
    xs = list(xs)
    k = 2
    while k <= n:
        j = k // 2
        while j >= 1:
            for i in range(n):
                l = i ^ j
                if l > i:
                    if (i & k) == 0:
                        _cmp_exchange(xs, i, l)
                    else:
                        _cmp_exchange(xs, l, i)
            j //= 2
        k *= 2
    return xs


def _bitonic_merge_desc(xs):
    n = len(xs)
    xs = list(xs)
    j = n // 2
    while j >= 1:
        for i in range(n):
            l = i ^ j
            if l > i:
                _cmp_exchange(xs, i, l)
        j //= 2
    return xs


def _top16_sorted(xs):
    n = PEER_TOPK
    xs = list(xs) + [None] * (n - len(xs))
    xs = _bitonic_sort_desc(xs)
    shift = SUBLANES // 2
    while shift >= 1:
        merged = []
        for i in range(n):
            a, b = xs[i], xs[n - 1 - i]
            if b is not None:
                b = pltpu.roll(b, shift, axis=0)
            merged.append(b if a is None else (a if b is None else jnp.maximum(a, b)))
        xs = _bitonic_merge_desc(merged)
        shift //= 2
    return xs


def _route_kernel(q_ref, keys_ref, s1_ref, s2_ref, e1_ref, e2_ref, tau_ref):
    q = q_ref[0]
    nk = PEER_NKEYS
    s1 = _dot3_nt(keys_ref[0, 0], q[:, :nk])
    s2 = _dot3_nt(keys_ref[0, 1], q[:, nk:])
    split = lambda s: [s[i * SUBLANES:(i + 1) * SUBLANES] for i in range(nk // SUBLANES)]
    v1 = _top16_sorted(split(s1))
    v2 = _top16_sorted(split(s2))
    sub = lax.broadcasted_iota(jnp.int32, v1[0].shape, 0)

    def spread(vals):
        out = vals[0]
        for j in range(1, SUBLANES):
            out = jnp.where(sub == j, vals[j], out)
        return out

    v2_lo, v2_hi, v1_hi = spread(v2[:SUBLANES]), spread(v2[SUBLANES:]), spread(v1[SUBLANES:])
    cands = [v1[0] + v2_lo, v1[0] + v2_hi]
    cands += [v1[i] + v2_lo for i in range(1, SUBLANES)]
    cands += [v1_hi + v2[0]]
    top = _top16_sorted(cands)
    m = top[0]
    zsum = jnp.zeros_like(m)
    for c in top:
        zsum = zsum + jnp.exp(c - m)
    s1_ref[0, 0] = s1
    s2_ref[0, 0] = s2
    e1_ref[0, 0] = jnp.exp(s1 - v1[0][0:1]) / zsum[0:1]
    e2_ref[0, 0] = jnp.exp(s2 - v2[0][0:1])
    tau_ref[0, 0] = top[PEER_TOPK - 1][0:1]


def _peer_route(q, keys):
    b, t, _ = q.shape
    tt = 512
    nk = PEER_NKEYS
    big = jax.ShapeDtypeStruct((b, PEER_HEADS, nk, t), F32)
    spec = pl.BlockSpec((1, 1, nk, tt), lambda bi, i, h: (bi, h, 0, i))
    return pl.pallas_call(
        _route_kernel,
        out_shape=(big, big, big, big, jax.ShapeDtypeStruct((b, PEER_HEADS, 1, t), F32)),
        grid=(b, t // tt, PEER_HEADS),
        in_specs=[
            pl.BlockSpec((1, tt, 2 * nk), lambda bi, i, h: (bi, i, h)),
            pl.BlockSpec((1, 2, nk, nk), lambda bi, i, h: (h, 0, 0, 0)),
        ],
        out_specs=(spec, spec, spec, spec,
                   pl.BlockSpec((1, 1, 1, tt), lambda bi, i, h: (bi, h, 0, i))),
        compiler_params=_cparams(("arbitrary", "arbitrary", "arbitrary")),
        name="peer_route",
    )(q, keys)


def _gelu(x):
    return 0.5 * x * (1.0 + lax.erf(x * (2.0 ** -0.5)))


def _peer_mix_kernel(h_ref, u_ref, vt_ref, s1_ref, s2_ref, e1_ref, e2_ref, tau_ref,
                     o_ref, st_ref, gt_ref, *, n_a, n_e):
    f = pl.program_id(0)
    nk = PEER_NKEYS
    slot_p = lax.rem(f, 2)
    slot_c = 1 - slot_p
    e_c = lax.rem(jnp.maximum(f - 1, 0), n_e)

    @pl.when(f == 0)
    def _():
        st_ref[1] = jnp.zeros(st_ref.shape[1:], F32)
        gt_ref[1] = jnp.zeros(gt_ref.shape[1:], F32)

    @pl.when(e_c == 0)
    def _():
        o_ref[...] = jnp.zeros(o_ref.shape, F32)

    a_t = (_gelu(st_ref[slot_c]) * gt_ref[slot_c]).astype(BF16)
    o_ref[0] += _dot(vt_ref[...], a_t)

    st_ref[slot_p] = _dot_nt(u_ref[...], h_ref[0])
    for al in range(n_a):
        gate = None
        for h in range(PEER_HEADS):
            cand = s1_ref[0, h, al:al + 1, :] + s2_ref[0, h]
            w = e1_ref[0, h, al:al + 1, :] * e2_ref[0, h]
            term = jnp.where(cand >= tau_ref[0, h], w, 0.0)
            gate = term if gate is None else gate + term
        gt_ref[slot_p, al * nk:(al + 1) * nk, :] = gate


def _peer_mix(h2, u_bf, vt_bf, s1, s2, e1, e2, tau):
    b, t, d = h2.shape
    nk = PEER_NKEYS
    n_exp = u_bf.shape[0]
    tt = 512
    n_a = 8
    eb = n_a * nk
    n_e = n_exp // eb
    n_t = t // tt
    total = b * n_t * n_e

    def split(idx):
        return idx // (n_t * n_e), lax.rem(idx // n_e, n_t), lax.rem(idx, n_e)

    prod = lambda f: split(jnp.minimum(f, total - 1))
    cons = lambda f: split(jnp.maximum(f - 1, 0))

    def tok_spec(rows):
        def imap(f):
            bi, ti, _ = prod(f)
            return (bi, 0, 0, ti)
        return pl.BlockSpec((1, PEER_HEADS, rows, tt), imap)

    def blk_spec():
        def imap(f):
            bi, ti, e = prod(f)
            return (bi, 0, e, ti)
        return pl.BlockSpec((1, PEER_HEADS, n_a, tt), imap)

    return pl.pallas_call(
        functools.partial(_peer_mix_kernel, n_a=n_a, n_e=n_e),
        out_shape=jax.ShapeDtypeStruct((b, d, t), F32),
        grid=(total + 1,),
        in_specs=[
            pl.BlockSpec((1, tt, d), lambda f: (prod(f)[0], prod(f)[1], 0)),
            pl.BlockSpec((eb, d), lambda f: (prod(f)[2], 0)),
            pl.BlockSpec((d, eb), lambda f: (0, cons(f)[2])),
            blk_spec(), tok_spec(nk), blk_spec(), tok_spec(nk), tok_spec(1),
        ],
        out_specs=pl.BlockSpec((1, d, tt), lambda f: (cons(f)[0], 0, cons(f)[1])),
        scratch_shapes=[pltpu.VMEM((2, eb, tt), F32), pltpu.VMEM((2, eb, tt), F32)],
        compiler_params=_cparams(("arbitrary",)),
        name="peer_mix",
    )(h2, u_bf, vt_bf, s1, s2, e1, e2, tau)


def _peer_out_kernel(acc_ref, x_ref, g_ref, o_ref):
    o_ref[0] = x_ref[0] + g_ref[0] * acc_ref[0].T


def _peer_residual(acc_t, x1, gate):
    b, t, d = x1.shape
    tt = 512
    return pl.pallas_call(
        _peer_out_kernel,
        out_shape=jax.ShapeDtypeStruct((b, t, d), F32),
        grid=(b, t // tt),
        in_specs=[
            pl.BlockSpec((1, d, tt), lambda bi, i: (bi, 0, i)),
            pl.BlockSpec((1, tt, d), lambda bi, i: (bi, i, 0)),
            pl.BlockSpec((1, 1, d), lambda bi, i: (bi, 0, 0)),
        ],
        out_specs=pl.BlockSpec((1, tt, d), lambda bi, i: (bi, i, 0)),
        compiler_params=_cparams(("arbitrary", "arbitrary")),
        name="peer_residual",
    )(acc_t, x1, gate)


def kernel(x, c, ctx, c_ctx, ada_w, ada_b, norm1_g, norm2_g, w_in, da_qn_g, da_kn_g, da_lam_q1, da_lam_k1, da_lam_q2, da_lam_k2, da_subln_g, gla_w2_f, gla_b_f, gla_w2_b, gla_b_b, gla_norm_g, w_br_a, w_br_b, w_out, peer_wq, peer_keys, peer_u, peer_v):
    b, t, d = x.shape
    t_ctx = ctx.shape[1]
    assert ada_w.shape[0] == 1, "single-layer kernel"
    assert t % ROW_TILE == 0 and t_ctx % ROW_TILE == 0 and t % GRID_W == 0
    l = 0

    cond = jnp.concatenate([c, c_ctx[None, :]], axis=0)
    mod = _ada_modulation(cond, ada_w[l], ada_b[l])[:b + 1].reshape(b + 1, 6, d)
    ss1 = mod[:, 0:2]
    g1 = mod[:b, 2:3]
    ss2 = mod[:b, 3:5]
    g2 = mod[:b, 5:6]

    h_all = _norm_mod_all(x, ctx, ss1, norm1_g[l].reshape(1, d))
    r = t_ctx + t
    wi = w_in[l]
    o_glr = 6144
    w_main = jnp.concatenate([wi[:, :o_glr], wi[:, o_glr + 2 * GLA_RANK:]], axis=1).astype(BF16)
    w_lr = jnp.pad(wi[:, o_glr:o_glr + 2 * GLA_RANK], ((0, 0), (0, LANES - 2 * GLA_RANK))).astype(BF16)
    h2d = h_all.reshape(b * r, d)
    n_main = w_main.shape[1]
    proj3 = _matmul(h2d, w_main, 512, 1024, BF16, "in_proj").reshape(b, r, n_main)
    lr3 = _matmul(h2d, w_lr, 512, LANES, F32, "in_proj_lr").reshape(b, r, LANES)

    cos_t, sa_t, sb_t = _rope_tables(t, t_ctx)
    qm = _qk_prep(proj3, cos_t, sa_t, sb_t, da_qn_g[l], 0, True, t_ctx)
    kh = _qk_prep(proj3, cos_t, sa_t, sb_t, da_kn_g[l], 8, False, t_ctx)
    lam_rows = jnp.zeros((8, LANES), F32)
    lam_rows = lam_rows.at[0, :DA_DH].set(da_lam_q1[l]).at[1, :DA_DH].set(da_lam_k1[l])
    lam_rows = lam_rows.at[2, :DA_DH].set(da_lam_q2[l]).at[3, :DA_DH].set(da_lam_k2[l])
    y_a = _diff_attention(qm, kh, proj3, 16, lam_rows, da_subln_g[l])

    def pad_w2(w2, row0):
        w = w2.reshape(GLA_RANK, GLA_HEADS, GLA_DK).transpose(1, 0, 2)
        return jnp.pad(w, ((0, 0), (row0, LANES - GLA_RANK - row0), (0, 0))).astype(F32)

    cols = (24, 28, 16, 20)
    o_f = _gla_direction(proj3, lr3, pad_w2(gla_w2_f[l], 0),
                         gla_b_f[l].reshape(GLA_HEADS, 1, GLA_DK), t_ctx, cols, False)
    y_b = _gla_direction(proj3, lr3, pad_w2(gla_w2_b[l], GLA_RANK),
                         gla_b_b[l].reshape(GLA_HEADS, 1, GLA_DK), t_ctx, cols, True,
                         o_fwd=o_f, norm_g=gla_norm_g[l])

    z = _merge(y_a, y_b, w_br_a[l].astype(BF16), w_br_b[l].astype(BF16), proj3, 6144, 8192, t_ctx)
    x1 = _out_proj_residual(z, w_out[l].astype(BF16), x, g1)

    h2, q = _peer_query(x1, ss2, norm2_g[l].reshape(1, d), peer_wq[l].astype(BF16))
    s1, s2, e1, e2, tau = _peer_route(q, peer_keys[l])
    acc_t = _peer_mix(h2, peer_u[l].astype(BF16), peer_v[l].T.astype(BF16), s1, s2, e1, e2, tau)
    return _peer_residual(acc_t, x1, g2)
```

```python
import functools
import math

import jax
import jax.numpy as jnp
import numpy as np
from jax import lax
from jax.experimental import pallas as pl
from jax.experimental.pallas import tpu as pltpu

F32 = jnp.float32
BF16 = jnp.bfloat16

GRID_W = 64
DA_HEADS = 8
DA_DH = 64
DA_DV = 128
GLA_HEADS = 4
GLA_DK = 128
GLA_DV = 256
GLA_RANK = 16
GLA_TAU = 16.0
GLA_CHUNK = 64
PEER_HEADS = 8
PEER_NKEYS = 128
PEER_TOPK = 16
ROPE_BASE = 10000.0
EPS = 1e-6
LAM_INIT = 0.8 - 0.6 * math.exp(-0.3 * 0)

LANES = 128
ROW_TILE = 256
VMEM_LIMIT = 56 * 1024 * 1024

NEG_INF = float("-inf")
LOG2_E = 1.0 / math.log(2.0)


def _cparams(sem):
    return pltpu.CompilerParams(dimension_semantics=sem, vmem_limit_bytes=VMEM_LIMIT)


def _split2(x):
    hi = x.astype(BF16)
    lo = (x - hi.astype(F32)).astype(BF16)
    return hi, lo


def _split3(x):
    p1 = x.astype(BF16)
    r1 = x - p1.astype(F32)
    p2 = r1.astype(BF16)
    p3 = (r1 - p2.astype(F32)).astype(BF16)
    return p1, p2, p3


def _dot(a, b):
    return jnp.dot(a, b, preferred_element_type=F32)


def _dot_nt(a, b):
    return lax.dot_general(a, b, (((1,), (1,)), ((), ())), preferred_element_type=F32)


def _dot_tn(a, b):
    return lax.dot_general(a, b, (((0,), (0,)), ((), ())), preferred_element_type=F32)


def _dot3_nt(a, b):
    ah, al = _split2(a)
    bh, bl = _split2(b)
    return _dot_nt(ah, bh) + _dot_nt(ah, bl) + _dot_nt(al, bh)


def _dot3(a, b):
    ah, al = _split2(a)
    bh, bl = _split2(b)
    return _dot(ah, bh) + _dot(ah, bl) + _dot(al, bh)


def _exact_lhs_dot(m_bf16, x):
    p1, p2, p3 = _split3(x)
    return _dot(m_bf16, p1) + _dot(m_bf16, p2) + _dot(m_bf16, p3)


def _ada_kernel(cb_ref, w_ref, b_ref, o_ref, *, n_rows):
    tn = w_ref.shape[1]
    o_ref[...] = jnp.zeros(o_ref.shape, F32)
    for r in range(n_rows):
        cv = cb_ref[r]
        s = cv * (1.0 / (1.0 + jnp.exp(-cv)))
        for j in range(tn // LANES):
            sl = slice(j * LANES, (j + 1) * LANES)
            col = jnp.sum(s * w_ref[:, sl], axis=0, keepdims=True)
            o_ref[r:r + 1, sl] = col + b_ref[:, sl]


def _ada_modulation(cond, ada_w, ada_b):
    n_rows, d = cond.shape
    n = ada_w.shape[1]
    tn = 1024
    cb = jnp.broadcast_to(cond[:, :, None], (n_rows, d, LANES))
    return pl.pallas_call(
        functools.partial(_ada_kernel, n_rows=n_rows),
        out_shape=jax.ShapeDtypeStruct((8, n), F32),
        grid=(n // tn,),
        in_specs=[
            pl.BlockSpec((n_rows, d, LANES), lambda j: (0, 0, 0)),
            pl.BlockSpec((d, tn), lambda j: (0, j)),
            pl.BlockSpec((1, tn), lambda j: (0, j)),
        ],
        out_specs=pl.BlockSpec((8, tn), lambda j: (0, j)),
        compiler_params=_cparams(("arbitrary",)),
        name="ada_modulation",
    )(cb, ada_w, ada_b.reshape(1, n))


def _rms_mod(xf, g, shift, scale):
    var = jnp.mean(xf * xf, axis=-1, keepdims=True)
    y = xf * lax.rsqrt(var + EPS) * g
    return y * (1.0 + scale) + shift


def _norm_mod_kernel(x_ref, ctx_ref, ss_ref, g_ref, o_ref, *, nc):
    i = pl.program_id(1)

    def emit(src):
        h = _rms_mod(src[0], g_ref[...], ss_ref[0, 0:1, :], ss_ref[0, 1:2, :])
        o_ref[0] = h.astype(o_ref.dtype)

    @pl.when(i < nc)
    def _():
        emit(ctx_ref)

    @pl.when(i >= nc)
    def _():
        emit(x_ref)


def _norm_mod_all(x, ctx, ss, g):
    b, t, d = x.shape
    tc = ctx.shape[1]
    tm = ROW_TILE
    nc = tc // tm
    r = tc + t
    return pl.pallas_call(
        functools.partial(_norm_mod_kernel, nc=nc),
        out_shape=jax.ShapeDtypeStruct((b, r, d), BF16),
        grid=(b, r // tm),
        in_specs=[
            pl.BlockSpec((1, tm, d), lambda bi, i: (bi, jnp.maximum(i - nc, 0), 0)),
            pl.BlockSpec((1, tm, d), lambda bi, i: (bi, jnp.minimum(i, nc - 1), 0)),
            pl.BlockSpec((1, 2, d), lambda bi, i: (jnp.where(i < nc, b, bi), 0, 0)),
            pl.BlockSpec((1, d), lambda bi, i: (0, 0)),
        ],
        out_specs=pl.BlockSpec((1, tm, d), lambda bi, i: (bi, i, 0)),
        compiler_params=_cparams(("arbitrary", "arbitrary")),
        name="norm_mod",
    )(x, ctx, ss, g)


def _matmul_kernel(a_ref, b_ref, o_ref):
    o_ref[...] = _dot(a_ref[...], b_ref[...]).astype(o_ref.dtype)


def _matmul(a, b, tm, tn, out_dtype, name):
    m, k = a.shape
    n = b.shape[1]
    return pl.pallas_call(
        _matmul_kernel,
        out_shape=jax.ShapeDtypeStruct((m, n), out_dtype),
        grid=(n // tn, m // tm),
        in_specs=[
            pl.BlockSpec((tm, k), lambda j, i: (i, 0)),
            pl.BlockSpec((k, tn), lambda j, i: (0, j)),
        ],
        out_specs=pl.BlockSpec((tm, tn), lambda j, i: (i, j)),
        compiler_params=_cparams(("arbitrary", "arbitrary")),
        name=name,
    )(a, b)


def _rope_tables(t_lat, t_ctx):
    rows = t_lat // GRID_W
    row = np.repeat(np.arange(rows), GRID_W).astype(np.float64)
    col = np.tile(np.arange(GRID_W), rows).astype(np.float64)
    half = DA_DH // 2
    inv = ROPE_BASE ** (-np.arange(0, half, 2, dtype=np.float64) / half)
    ar = row[:, None] * inv
    ac = col[:, None] * inv
    cr, sr, cc, sc = np.cos(ar), np.sin(ar), np.cos(ac), np.sin(ac)
    z = np.zeros_like(sr)
    cos64 = np.concatenate([cr, cr, cc, cc], axis=1)
    sa64 = np.concatenate([-sr, z, -sc, z], axis=1)
    sb64 = np.concatenate([z, sr, z, sc], axis=1)
    tile2 = lambda a: np.concatenate([a, a], axis=1)
    pad = lambda a, v: jnp.asarray(
        np.concatenate([np.full((t_ctx, LANES), v), tile2(a)], axis=0), F32)
    return pad(cos64, 1.0), pad(sa64, 0.0), pad(sb64, 0.0)


def _qk_prep_kernel(x_ref, cos_ref, sa_ref, sb_ref, g_ref, bd_ref, o_ref, *, is_q):
    lane = lax.broadcasted_iota(jnp.int32, cos_ref.shape, 1)
    for h in range(DA_HEADS):
        x = x_ref[0, :, h * LANES:(h + 1) * LANES].astype(F32)
        ss = _exact_lhs_dot_rhs(x * x, bd_ref[...])
        y = x * lax.rsqrt(ss * (1.0 / DA_DH) + EPS) * g_ref[...]
        up = pltpu.roll(y, LANES - 16, axis=1)
        dn = pltpu.roll(y, 16, axis=1)
        o = y * cos_ref[...] + up * sa_ref[...] + dn * sb_ref[...]
        if is_q:
            o = o * (DA_DH ** -0.5 * LOG2_E)
            zero = jnp.zeros_like(o)
            o_ref[0, :, 2 * h * LANES:(2 * h + 1) * LANES] = jnp.where(lane < DA_DH, o, zero).astype(o_ref.dtype)
            o_ref[0, :, (2 * h + 1) * LANES:(2 * h + 2) * LANES] = jnp.where(lane >= DA_DH, o, zero).astype(o_ref.dtype)
        else:
            o_ref[0, :, h * LANES:(h + 1) * LANES] = o.astype(o_ref.dtype)


def _exact_lhs_dot_rhs(x, m_bf16):
    p1, p2, p3 = _split3(x)
    return _dot(p1, m_bf16) + _dot(p2, m_bf16) + _dot(p3, m_bf16)


def _qk_prep(proj3, cos_t, sa_t, sb_t, gain, col0, is_q, t_ctx):
    b, r, _ = proj3.shape
    tm = ROW_TILE
    row0 = t_ctx // tm if is_q else 0
    n_rows = r - t_ctx if is_q else r
    width = DA_HEADS * (2 * LANES if is_q else LANES)
    bd = jnp.asarray(np.kron(np.eye(2), np.ones((DA_DH, DA_DH))), BF16)
    g2 = jnp.concatenate([gain, gain]).reshape(1, LANES).astype(F32)
    return pl.pallas_call(
        functools.partial(_qk_prep_kernel, is_q=is_q),
        out_shape=jax.ShapeDtypeStruct((b, n_rows, width), BF16),
        grid=(b, n_rows // tm),
        in_specs=[
            pl.BlockSpec((1, tm, DA_HEADS * LANES), lambda bi, i: (bi, row0 + i, col0 // DA_HEADS)),
            pl.BlockSpec((tm, LANES), lambda bi, i: (row0 + i, 0)),
            pl.BlockSpec((tm, LANES), lambda bi, i: (row0 + i, 0)),
            pl.BlockSpec((tm, LANES), lambda bi, i: (row0 + i, 0)),
            pl.BlockSpec((1, LANES), lambda bi, i: (0, 0)),
            pl.BlockSpec((LANES, LANES), lambda bi, i: (0, 0)),
        ],
        out_specs=pl.BlockSpec((1, tm, width), lambda bi, i: (bi, i, 0)),
        compiler_params=_cparams(("arbitrary", "arbitrary")),
        name="q_prep" if is_q else "k_prep",
    )(proj3, cos_t, sa_t, sb_t, g2, bd)


def _attn_body(q_ref, k_ref, v_ref, lam_ref, g_ref, o_ref, sp_ref, mp_ref, sc_ref, mc_ref, *, tk, nk):
    tq = q_ref.shape[1]
    q = q_ref[0]
    qs = jnp.concatenate([q[:, :LANES], q[:, LANES:]], axis=0)
    m_c = mc_ref[:, 0:1]
    ones = jnp.ones((tk, LANES), BF16)
    acc = jnp.zeros((2 * tq, 2 * LANES), F32)
    mpart = None
    for c in range(nk):
        sl = slice(c * tk, (c + 1) * tk)
        s = _dot_nt(qs, k_ref[0, sl, :])
        sp_ref[:, sl] = s
        for j in range(tk // LANES):
            blk = s[:, j * LANES:(j + 1) * LANES]
            mpart = blk if mpart is None else jnp.maximum(mpart, blk)
        p = jnp.exp2((sc_ref[:, sl] - m_c).astype(BF16))
        vaug = jnp.concatenate([v_ref[0, sl, :], ones], axis=1)
        acc = acc + _dot(p, vaug)
    mp_ref[...] = jnp.broadcast_to(jnp.max(mpart, axis=1, keepdims=True), mp_ref.shape)
    ratio = acc[:, :LANES] / acc[:, LANES:]
    lam = (jnp.exp(jnp.sum(lam_ref[0:1, :] * lam_ref[1:2, :], axis=1, keepdims=True))
           - jnp.exp(jnp.sum(lam_ref[2:3, :] * lam_ref[3:4, :], axis=1, keepdims=True)) + LAM_INIT)
    o = ratio[:tq] - lam * ratio[tq:]
    var = jnp.mean(o * o, axis=-1, keepdims=True)
    o = o * lax.rsqrt(var + EPS) * g_ref[...] * (1.0 - LAM_INIT)
    o_ref[0] = o.astype(o_ref.dtype)


def _attn_kernel(q_ref, k_ref, v_ref, lam_ref, g_ref, o_ref, sa_ref, ma_ref, sb_ref, mb_ref, *, tk, nk):
    f = pl.program_id(0)
    body = functools.partial(_attn_body, q_ref, k_ref, v_ref, lam_ref, g_ref, o_ref, tk=tk, nk=nk)

    @pl.when(f == 0)
    def _():
        sb_ref[...] = jnp.zeros(sb_ref.shape, F32)
        mb_ref[...] = jnp.zeros(mb_ref.shape, F32)

    @pl.when(lax.rem(f, 2) == 0)
    def _():
        body(sa_ref, ma_ref, sb_ref, mb_ref)

    @pl.when(lax.rem(f, 2) == 1)
    def _():
        body(sb_ref, mb_ref, sa_ref, ma_ref)


def _diff_attention(qm, kh, proj3, v_col0, lam_rows, subln_g):
    b, t, _ = qm.shape
    r = kh.shape[1]
    tq = ROW_TILE
    tk = ROW_TILE
    n_q = t // tq
    total = b * DA_HEADS * n_q

    def split(idx):
        return idx // (DA_HEADS * n_q), lax.rem(idx // n_q, DA_HEADS), lax.rem(idx, n_q)

    prod = lambda f: split(jnp.minimum(f, total - 1))
    cons = lambda f: split(jnp.maximum(f - 1, 0))
    return pl.pallas_call(
        functools.partial(_attn_kernel, tk=tk, nk=r // tk),
        out_shape=jax.ShapeDtypeStruct((b, t, DA_HEADS * DA_DV), BF16),
        grid=(total + 1,),
        in_specs=[
            pl.BlockSpec((1, tq, 2 * LANES), lambda f: (prod(f)[0], prod(f)[2], prod(f)[1])),
            pl.BlockSpec((1, r, LANES), lambda f: (prod(f)[0], 0, prod(f)[1])),
            pl.BlockSpec((1, r, LANES), lambda f: (cons(f)[0], 0, v_col0 + cons(f)[1])),
            pl.BlockSpec((8, LANES), lambda f: (0, 0)),
            pl.BlockSpec((1, LANES), lambda f: (0, 0)),
        ],
        out_specs=pl.BlockSpec((1, tq, DA_DV), lambda f: (cons(f)[0], cons(f)[2], cons(f)[1])),
        scratch_shapes=[pltpu.VMEM((2 * tq, r), F32), pltpu.VMEM((2 * tq, LANES), F32),
                        pltpu.VMEM((2 * tq, r), F32), pltpu.VMEM((2 * tq, LANES), F32)],
        compiler_params=_cparams(("arbitrary",)),
        name="diff_attention",
    )(qm, kh, proj3, lam_rows, subln_g.reshape(1, DA_DV).astype(F32))


def _log_sigmoid(z):
    return jnp.minimum(z, 0.0) - jnp.log(1.0 + jnp.exp(-jnp.abs(z)))


def _gla_kernel(*refs, reverse, final, n_chunks):
    if final:
        (q_ref, k_ref, v_ref, lr_ref, w2_ref, b_ref, tri_ref, ones_ref,
         of_ref, gr_ref, ng_ref, o_ref, st_ref) = refs
    else:
        (q_ref, k_ref, v_ref, lr_ref, w2_ref, b_ref, tri_ref, ones_ref, o_ref, st_ref) = refs
    c_len = GLA_CHUNK

    @pl.when(pl.program_id(2) == 0)
    def _():
        st_ref[...] = jnp.zeros(st_ref.shape, F32)

    q = q_ref[0].astype(F32) * (GLA_DK ** -0.5)
    k = k_ref[0].astype(F32)
    v = v_ref[0]
    z = _dot3(lr_ref[0], w2_ref[0]) + b_ref[0]
    la = _log_sigmoid(z) * (1.0 / GLA_TAU)
    tri = tri_ref[...]
    cum = _exact_lhs_dot(tri, la)
    tot = _exact_lhs_dot(ones_ref[...], la)
    q_in = (q * jnp.exp(cum)).astype(BF16)
    k_in = (k * jnp.exp(-cum)).astype(BF16)
    k_st = (k * jnp.exp(tot - cum)).astype(BF16)
    dec = jnp.exp(tot)
    att = jnp.where(tri > 0, _dot_nt(q_in, k_in), 0.0)
    o_intra = _dot(att.astype(BF16), v)

    order = range(n_chunks - 1, -1, -1) if reverse else range(n_chunks)
    outs = [None] * n_chunks
    for c in order:
        sl = slice(c * c_len, (c + 1) * c_len)
        st = st_ref[...]
        outs[c] = o_intra[sl] + _dot_nt(q_in[sl], st.astype(BF16))
        kv = _dot_tn(v[sl], k_st[sl])
        st_ref[...] = st * dec[c * c_len:c * c_len + 1, :] + kv
    o = jnp.concatenate(outs, axis=0)
    if final:
        o = o + of_ref[0].astype(F32)
        var = jnp.mean(o * o, axis=-1, keepdims=True)
        y = o * lax.rsqrt(var + EPS) * ng_ref[...]
        gr = gr_ref[0].astype(F32)
        y = y * (gr * (1.0 / (1.0 + jnp.exp(-gr))))
        o_ref[0] = y.astype(o_ref.dtype)
    else:
        o_ref[0] = o.astype(o_ref.dtype)


def _gla_direction(proj3, lr3, w2p, bias, t_ctx, cols, reverse, o_fwd=None, norm_g=None):
    b, r, _ = proj3.shape
    tc = ROW_TILE
    n_chunks = tc // GLA_CHUNK
    ng = r // tc
    ncg = t_ctx // tc
    final = o_fwd is not None
    q_col, k_col, v_col, gr_col = cols

    blk = np.kron(np.eye(n_chunks), np.ones((GLA_CHUNK, GLA_CHUNK)))
    full_tri = np.tril(np.ones((tc, tc)))
    tri = blk * (full_tri.T if reverse else full_tri)
    tri = jnp.asarray(tri, BF16)
    ones_bd = jnp.asarray(blk, BF16)

    if reverse:
        def grp(s):
            return jnp.where(s < ncg, ncg - 1 - s, ng - 1 - (s - ncg))
    else:
        def grp(s):
            return s

    def lat(s):
        return jnp.maximum(grp(jnp.maximum(s, ncg)) - ncg, 0)

    in_specs = [
        pl.BlockSpec((1, tc, GLA_DK), lambda bi, h, s: (bi, grp(s), q_col + h)),
        pl.BlockSpec((1, tc, GLA_DK), lambda bi, h, s: (bi, grp(s), k_col + h)),
        pl.BlockSpec((1, tc, GLA_DV), lambda bi, h, s: (bi, grp(s), v_col + h)),
        pl.BlockSpec((1, tc, LANES), lambda bi, h, s: (bi, grp(s), 0)),
        pl.BlockSpec((1, LANES, GLA_DK), lambda bi, h, s: (h, 0, 0)),
        pl.BlockSpec((1, 1, GLA_DK), lambda bi, h, s: (h, 0, 0)),
        pl.BlockSpec((tc, tc), lambda bi, h, s: (0, 0)),
        pl.BlockSpec((tc, tc), lambda bi, h, s: (0, 0)),
    ]
    args = [proj3, proj3, proj3, lr3, w2p, bias, tri, ones_bd]
    if final:
        in_specs += [
            pl.BlockSpec((1, tc, GLA_DV), lambda bi, h, s: (bi, lat(s), h)),
            pl.BlockSpec((1, tc, GLA_DV), lambda bi, h, s: (bi, grp(s), gr_col + h)),
            pl.BlockSpec((1, GLA_DV), lambda bi, h, s: (0, 0)),
        ]
        args += [o_fwd, proj3, norm_g.reshape(1, GLA_DV).astype(F32)]
    return pl.pallas_call(
        functools.partial(_gla_kernel, reverse=reverse, final=final, n_chunks=n_chunks),
        out_shape=jax.ShapeDtypeStruct((b, r - t_ctx, GLA_HEADS * GLA_DV), BF16 if final else F32),
        grid=(b, GLA_HEADS, ng),
        in_specs=in_specs,
        out_specs=pl.BlockSpec((1, tc, GLA_DV), lambda bi, h, s: (bi, lat(s), h)),
        scratch_shapes=[pltpu.VMEM((GLA_DV, GLA_DK), F32)],
        compiler_params=_cparams(("arbitrary", "arbitrary", "arbitrary")),
        name="gla_bwd" if reverse else "gla_fwd",
    )(*args)


def _sigmoid(x):
    return 1.0 / (1.0 + jnp.exp(-x))


def _merge_kernel(ya_ref, yb_ref, wa_ref, wb_ref, ga_ref, gb_ref, o_ref):
    a = _dot(ya_ref[0], wa_ref[...])
    bb = _dot(yb_ref[0], wb_ref[...])
    z = _sigmoid(ga_ref[0].astype(F32)) * a + _sigmoid(gb_ref[0].astype(F32)) * bb
    o_ref[0] = z.astype(o_ref.dtype)


def _merge(y_a, y_b, wa, wb, proj3, ga_col0, gb_col0, t_ctx):
    b, t, ka = y_a.shape
    kb = y_b.shape[2]
    d = wa.shape[1]
    tm, tn = ROW_TILE, 1024
    row0 = t_ctx // tm
    return pl.pallas_call(
        _merge_kernel,
        out_shape=jax.ShapeDtypeStruct((b, t, d), BF16),
        grid=(d // tn, b, t // tm),
        in_specs=[
            pl.BlockSpec((1, tm, ka), lambda j, bi, i: (bi, i, 0)),
            pl.BlockSpec((1, tm, kb), lambda j, bi, i: (bi, i, 0)),
            pl.BlockSpec((ka, tn), lambda j, bi, i: (0, j)),
            pl.BlockSpec((kb, tn), lambda j, bi, i: (0, j)),
            pl.BlockSpec((1, tm, tn), lambda j, bi, i: (bi, row0 + i, ga_col0 // tn + j)),
            pl.BlockSpec((1, tm, tn), lambda j, bi, i: (bi, row0 + i, gb_col0 // tn + j)),
        ],
        out_specs=pl.BlockSpec((1, tm, tn), lambda j, bi, i: (bi, i, j)),
        compiler_params=_cparams(("arbitrary", "arbitrary", "arbitrary")),
        name="merge",
    )(y_a, y_b, wa, wb, proj3, proj3)


def _out_proj_kernel(z_ref, w_ref, x_ref, g_ref, o_ref):
    y = _dot(z_ref[0], w_ref[...])
    o_ref[0] = x_ref[0] + g_ref[0] * y


def _out_proj_residual(z, w, x, gate):
    b, t, d = x.shape
    tm, tn = 512, 1024
    return pl.pallas_call(
        _out_proj_kernel,
        out_shape=jax.ShapeDtypeStruct((b, t, d), F32),
        grid=(d // tn, b, t // tm),
        in_specs=[
            pl.BlockSpec((1, tm, d), lambda j, bi, i: (bi, i, 0)),
            pl.BlockSpec((d, tn), lambda j, bi, i: (0, j)),
            pl.BlockSpec((1, tm, tn), lambda j, bi, i: (bi, i, j)),
            pl.BlockSpec((1, 1, tn), lambda j, bi, i: (bi, 0, j)),
        ],
        out_specs=pl.BlockSpec((1, tm, tn), lambda j, bi, i: (bi, i, j)),
        compiler_params=_cparams(("arbitrary", "arbitrary", "arbitrary")),
        name="out_proj",
    )(z, w, x, gate)


def _peer_q_kernel(x_ref, ss_ref, g_ref, w_ref, h_ref, q_ref):
    h = _rms_mod(x_ref[0], g_ref[...], ss_ref[0, 0:1, :], ss_ref[0, 1:2, :]).astype(BF16)
    h_ref[0] = h
    q_ref[0] = _dot(h, w_ref[...])


def _peer_query(x1, ss, g, wq):
    b, t, d = x1.shape
    n = wq.shape[1]
    tm = ROW_TILE
    return pl.pallas_call(
        _peer_q_kernel,
        out_shape=(jax.ShapeDtypeStruct((b, t, d), BF16), jax.ShapeDtypeStruct((b, t, n), F32)),
        grid=(b, t // tm),
        in_specs=[
            pl.BlockSpec((1, tm, d), lambda bi, i: (bi, i, 0)),
            pl.BlockSpec((1, 2, d), lambda bi, i: (bi, 0, 0)),
            pl.BlockSpec((1, d), lambda bi, i: (0, 0)),
            pl.BlockSpec((d, n), lambda bi, i: (0, 0)),
        ],
        out_specs=(pl.BlockSpec((1, tm, d), lambda bi, i: (bi, i, 0)),
                   pl.BlockSpec((1, tm, n), lambda bi, i: (bi, i, 0))),
        compiler_params=_cparams(("arbitrary", "arbitrary")),
        name="peer_query",
    )(x1, ss, g, wq)


SUBLANES = 8


def _cmp_exchange(xs, hi, lo):
    a, b = xs[hi], xs[lo]
    if a is None and b is None:
        return
    if a is None or b is None:
        xs[hi], xs[lo] = (b if a is None else a), None
        return
    xs[hi], xs[lo] = jnp.maximum(a, b), jnp.minimum(a, b)


def _bitonic_sort_desc(xs):
    n = len(xs)
    xs = list(xs)
    k = 2
    while k <= n:
        j = k // 2
        while j >= 1:
            for i in range(n):
                l = i ^ j
                if l > i:
                    if (i & k) == 0:
                        _cmp_exchange(xs, i, l)
                    else:
                        _cmp_exchange(xs, l, i)
            j //= 2
        k *= 2
    return xs


def _bitonic_merge_desc(xs):
    n = len(xs)
    xs = list(xs)
    j = n // 2
    while j >= 1:
        for i in range(n):
            l = i ^ j
            if l > i:
                _cmp_exchange(xs, i, l)
        j //= 2
    return xs


def _top16_sorted(xs):
    n = PEER_TOPK
    xs = list(xs) + [None] * (n - len(xs))
    xs = _bitonic_sort_desc(xs)
    shift = SUBLANES // 2
    while shift >= 1:
        merged = []
        for i in range(n):
            a, b = xs[i], xs[n - 1 - i]
            if b is not None:
                b = pltpu.roll(b, shift, axis=0)
            merged.append(b if a is None else (a if b is None else jnp.maximum(a, b)))
        xs = _bitonic_merge_desc(merged)
        shift //= 2
    return xs


def _route_kernel(q_ref, keys_ref, s1_ref, s2_ref, tau_ref):
    q = q_ref[0]
    nk = PEER_NKEYS
    s1 = _dot3_nt(keys_ref[0, 0], q[:, :nk]) * LOG2_E
    s2 = _dot3_nt(keys_ref[0, 1], q[:, nk:]) * LOG2_E
    split = lambda s: [s[i * SUBLANES:(i + 1) * SUBLANES] for i in range(nk // SUBLANES)]
    v1 = _top16_sorted(split(s1))
    v2 = _top16_sorted(split(s2))
    sub = lax.broadcasted_iota(jnp.int32, v1[0].shape, 0)

    def spread(vals):
        out = vals[0]
        for j in range(1, SUBLANES):
            out = jnp.where(sub == j, vals[j], out)
        return out

    v2_lo, v2_hi = spread(v2[:SUBLANES]), spread(v2[SUBLANES:])

    def top_pairs(w1):
        cands = [w1[0] + v2_lo, w1[0] + v2_hi]
        cands += [w1[i] + v2_lo for i in range(1, SUBLANES)]
        cands += [spread(w1[SUBLANES:]) + v2[0]]
        return _top16_sorted(cands)

    top = top_pairs(v1)
    m = top[0]
    zsum = jnp.zeros_like(m)
    for c in top:
        zsum = zsum + jnp.exp2(c - m)
    off = m + jnp.log2(zsum)
    tau = top_pairs([v - off for v in v1])[PEER_TOPK - 1]
    s1_ref[0, 0] = s1 - off[0:1]
    s2_ref[0, 0] = s2
    tau_ref[0, 0] = tau[0:1]


def _peer_route(q, keys):
    b, t, _ = q.shape
    tt = 512
    nk = PEER_NKEYS
    big = jax.ShapeDtypeStruct((b, PEER_HEADS, nk, t), F32)
    spec = pl.BlockSpec((1, 1, nk, tt), lambda bi, i, h: (bi, h, 0, i))
    return pl.pallas_call(
        _route_kernel,
        out_shape=(big, big, jax.ShapeDtypeStruct((b, PEER_HEADS, 1, t), F32)),
        grid=(b, t // tt, PEER_HEADS),
        in_specs=[
            pl.BlockSpec((1, tt, 2 * nk), lambda bi, i, h: (bi, i, h)),
            pl.BlockSpec((1, 2, nk, nk), lambda bi, i, h: (h, 0, 0, 0)),
        ],
        out_specs=(spec, spec,
                   pl.BlockSpec((1, 1, 1, tt), lambda bi, i, h: (bi, h, 0, i))),
        compiler_params=_cparams(("arbitrary", "arbitrary", "arbitrary")),
        name="peer_route",
    )(q, keys)


def _gelu(x):
    return 0.5 * x * (1.0 + lax.erf(x * (2.0 ** -0.5)))


def _peer_mix_kernel(h_ref, u_ref, vt_ref, s1_ref, s2_ref, tau_ref,
                     o_ref, st_ref, gt_ref, *, n_a, n_e):
    f = pl.program_id(0)
    nk = PEER_NKEYS
    slot_p = lax.rem(f, 2)
    slot_c = 1 - slot_p
    e_c = lax.rem(jnp.maximum(f - 1, 0), n_e)

    @pl.when(f == 0)
    def _():
        st_ref[1] = jnp.zeros(st_ref.shape[1:], F32)
        gt_ref[1] = jnp.zeros(gt_ref.shape[1:], F32)

    @pl.when(e_c == 0)
    def _():
        o_ref[...] = jnp.zeros(o_ref.shape, F32)

    a_t = (_gelu(st_ref[slot_c]) * gt_ref[slot_c]).astype(BF16)
    o_ref[0] += _dot(vt_ref[...], a_t)

    st_ref[slot_p] = _dot_nt(u_ref[...], h_ref[0])
    for al in range(n_a):
        gate = None
        for h in range(PEER_HEADS):
            cand = s1_ref[0, h, al:al + 1, :] + s2_ref[0, h]
            term = jnp.where(cand >= tau_ref[0, h], jnp.exp2(cand), 0.0)
            gate = term if gate is None else gate + term
        gt_ref[slot_p, al * nk:(al + 1) * nk, :] = gate


def _peer_mix(h2, u_bf, vt_bf, s1, s2, tau):
    b, t, d = h2.shape
    nk = PEER_NKEYS
    n_exp = u_bf.shape[0]
    tt = 512
    n_a = 8
    eb = n_a * nk
    n_e = n_exp // eb
    n_t = t // tt
    total = b * n_t * n_e

    def split(idx):
        return idx // (n_t * n_e), lax.rem(idx // n_e, n_t), lax.rem(idx, n_e)

    prod = lambda f: split(jnp.minimum(f, total - 1))
    cons = lambda f: split(jnp.maximum(f - 1, 0))

    def tok_spec(rows):
        def imap(f):
            bi, ti, _ = prod(f)
            return (bi, 0, 0, ti)
        return pl.BlockSpec((1, PEER_HEADS, rows, tt), imap)

    def blk_spec():
        def imap(f):
            bi, ti, e = prod(f)
            return (bi, 0, e, ti)
        return pl.BlockSpec((1, PEER_HEADS, n_a, tt), imap)

    return pl.pallas_call(
        functools.partial(_peer_mix_kernel, n_a=n_a, n_e=n_e),
        out_shape=jax.ShapeDtypeStruct((b, d, t), F32),
        grid=(total + 1,),
        in_specs=[
            pl.BlockSpec((1, tt, d), lambda f: (prod(f)[0], prod(f)[1], 0)),
            pl.BlockSpec((eb, d), lambda f: (prod(f)[2], 0)),
            pl.BlockSpec((d, eb), lambda f: (0, cons(f)[2])),
            blk_spec(), tok_spec(nk), tok_spec(1),
        ],
        out_specs=pl.BlockSpec((1, d, tt), lambda f: (cons(f)[0], 0, cons(f)[1])),
        scratch_shapes=[pltpu.VMEM((2, eb, tt), F32), pltpu.VMEM((2, eb, tt), F32)],
        compiler_params=_cparams(("arbitrary",)),
        name="peer_mix",
    )(h2, u_bf, vt_bf, s1, s2, tau)


def _peer_out_kernel(acc_ref, x_ref, g_ref, o_ref):
    o_ref[0] = x_ref[0] + g_ref[0] * acc_ref[0].T


def _peer_residual(acc_t, x1, gate):
    b, t, d = x1.shape
    tt = 512
    return pl.pallas_call(
        _peer_out_kernel,
        out_shape=jax.ShapeDtypeStruct((b, t, d), F32),
        grid=(b, t // tt),
        in_specs=[
            pl.BlockSpec((1, d, tt), lambda bi, i: (bi, 0, i)),
            pl.BlockSpec((1, tt, d), lambda bi, i: (bi, i, 0)),
            pl.BlockSpec((1, 1, d), lambda bi, i: (bi, 0, 0)),
        ],
        out_specs=pl.BlockSpec((1, tt, d), lambda bi, i: (bi, i, 0)),
        compiler_params=_cparams(("arbitrary", "arbitrary")),
        name="peer_residual",
    )(acc_t, x1, gate)


def kernel(x, c, ctx, c_ctx, ada_w, ada_b, norm1_g, norm2_g, w_in, da_qn_g, da_kn_g, da_lam_q1, da_lam_k1, da_lam_q2, da_lam_k2, da_subln_g, gla_w2_f, gla_b_f, gla_w2_b, gla_b_b, gla_norm_g, w_br_a, w_br_b, w_out, peer_wq, peer_keys, peer_u, peer_v):
    b, t, d = x.shape
    t_ctx = ctx.shape[1]
    assert ada_w.shape[0] == 1, "single-layer kernel"
    assert t % ROW_TILE == 0 and t_ctx % ROW_TILE == 0 and t % GRID_W == 0
    l = 0

    cond = jnp.concatenate([c, c_ctx[None, :]], axis=0)
    mod = _ada_modulation(cond, ada_w[l], ada_b[l])[:b + 1].reshape(b + 1, 6, d)
    ss1 = mod[:, 0:2]
    g1 = mod[:b, 2:3]
    ss2 = mod[:b, 3:5]
    g2 = mod[:b, 5:6]

    h_all = _norm_mod_all(x, ctx, ss1, norm1_g[l].reshape(1, d))
    r = t_ctx + t
    wi = w_in[l]
    o_glr = 6144
    w_main = jnp.concatenate([wi[:, :o_glr], wi[:, o_glr + 2 * GLA_RANK:]], axis=1).astype(BF16)
    w_lr = jnp.pad(wi[:, o_glr:o_glr + 2 * GLA_RANK], ((0, 0), (0, LANES - 2 * GLA_RANK))).astype(BF16)
    h2d = h_all.reshape(b * r, d)
    n_main = w_main.shape[1]
    proj3 = _matmul(h2d, w_main, 512, 1024, BF16, "in_proj").reshape(b, r, n_main)
    lr3 = _matmul(h2d, w_lr, 512, LANES, F32, "in_proj_lr").reshape(b, r, LANES)

    cos_t, sa_t, sb_t = _rope_tables(t, t_ctx)
    qm = _qk_prep(proj3, cos_t, sa_t, sb_t, da_qn_g[l], 0, True, t_ctx)
    kh = _qk_prep(proj3, cos_t, sa_t, sb_t, da_kn_g[l], 8, False, t_ctx)
    lam_rows = jnp.zeros((8, LANES), F32)
    lam_rows = lam_rows.at[0, :DA_DH].set(da_lam_q1[l]).at[1, :DA_DH].set(da_lam_k1[l])
    lam_rows = lam_rows.at[2, :DA_DH].set(da_lam_q2[l]).at[3, :DA_DH].set(da_lam_k2[l])
    y_a = _diff_attention(qm, kh, proj3, 16, lam_rows, da_subln_g[l])

    def pad_w2(w2, row0):
        w = w2.reshape(GLA_RANK, GLA_HEADS, GLA_DK).transpose(1, 0, 2)
        return jnp.pad(w, ((0, 0), (row0, LANES - GLA_RANK - row0), (0, 0))).astype(F32)

    cols = (24, 28, 16, 20)
    o_f = _gla_direction(proj3, lr3, pad_w2(gla_w2_f[l], 0),
                         gla_b_f[l].reshape(GLA_HEADS, 1, GLA_DK), t_ctx, cols, False)
    y_b = _gla_direction(proj3, lr3, pad_w2(gla_w2_b[l], GLA_RANK),
                         gla_b_b[l].reshape(GLA_HEADS, 1, GLA_DK), t_ctx, cols, True,
                         o_fwd=o_f, norm_g=gla_norm_g[l])

    z = _merge(y_a, y_b, w_br_a[l].astype(BF16), w_br_b[l].astype(BF16), proj3, 6144, 8192, t_ctx)
    x1 = _out_proj_residual(z, w_out[l].astype(BF16), x, g1)

    h2, q = _peer_query(x1, ss2, norm2_g[l].reshape(1, d), peer_wq[l].astype(BF16))
    s1, s2, tau = _peer_route(q, peer_keys[l])
    acc_t = _peer_mix(h2, peer_u[l].astype(BF16), peer_v[l].T.astype(BF16), s1, s2, tau)
    return _peer_residual(acc_t, x1, g2)
```

```python
import functools
import math

import jax
import jax.numpy as jnp
import numpy as np
from jax import lax
from jax.experimental import pallas as pl
from jax.experimental.pallas import tpu as pltpu

F32 = jnp.float32
BF16 = jnp.bfloat16

GRID_W = 64
DA_HEADS = 8
DA_DH = 64
DA_DV = 128
GLA_HEADS = 4
GLA_DK = 128
GLA_DV = 256
GLA_RANK = 16
GLA_TAU = 16.0
GLA_CHUNK = 64
PEER_HEADS = 8
PEER_NKEYS = 128
PEER_TOPK = 16
ROPE_BASE = 10000.0
EPS = 1e-6
LAM_INIT = 0.8 - 0.6 * math.exp(-0.3 * 0)

LANES = 128
ROW_TILE = 256
VMEM_LIMIT = 56 * 1024 * 1024

NEG_INF = float("-inf")
LOG2_E = 1.0 / math.log(2.0)


def _cparams(sem):
    return pltpu.CompilerParams(dimension_semantics=sem, vmem_limit_bytes=VMEM_LIMIT)


def _split2(x):
    hi = x.astype(BF16)
    lo = (x - hi.astype(F32)).astype(BF16)
    return hi, lo


def _split3(x):
    p1 = x.astype(BF16)
    r1 = x - p1.astype(F32)
    p2 = r1.astype(BF16)
    p3 = (r1 - p2.astype(F32)).astype(BF16)
    return p1, p2, p3


def _dot(a, b):
    return jnp.dot(a, b, preferred_element_type=F32)


def _dot_nt(a, b):
    return lax.dot_general(a, b, (((1,), (1,)), ((), ())), preferred_element_type=F32)


def _dot_tn(a, b):
    return lax.dot_general(a, b, (((0,), (0,)), ((), ())), preferred_element_type=F32)


def _dot3_nt(a, b):
    ah, al = _split2(a)
    bh, bl = _split2(b)
    return _dot_nt(ah, bh) + _dot_nt(ah, bl) + _dot_nt(al, bh)


def _dot3(a, b):
    ah, al = _split2(a)
    bh, bl = _split2(b)
    return _dot(ah, bh) + _dot(ah, bl) + _dot(al, bh)


def _exact_lhs_dot(m_bf16, x):
    p1, p2, p3 = _split3(x)
    return _dot(m_bf16, p1) + _dot(m_bf16, p2) + _dot(m_bf16, p3)


def _ada_kernel(cb_ref, w_ref, b_ref, o_ref, *, n_rows):
    tn = w_ref.shape[1]
    o_ref[...] = jnp.zeros(o_ref.shape, F32)
    for r in range(n_rows):
        cv = cb_ref[r]
        s = cv * (1.0 / (1.0 + jnp.exp(-cv)))
        for j in range(tn // LANES):
            sl = slice(j * LANES, (j + 1) * LANES)
            col = jnp.sum(s * w_ref[:, sl], axis=0, keepdims=True)
            o_ref[r:r + 1, sl] = col + b_ref[:, sl]


def _ada_modulation(cond, ada_w, ada_b):
    n_rows, d = cond.shape
    n = ada_w.shape[1]
    tn = 1024
    cb = jnp.broadcast_to(cond[:, :, None], (n_rows, d, LANES))
    return pl.pallas_call(
        functools.partial(_ada_kernel, n_rows=n_rows),
        out_shape=jax.ShapeDtypeStruct((8, n), F32),
        grid=(n // tn,),
        in_specs=[
            pl.BlockSpec((n_rows, d, LANES), lambda j: (0, 0, 0)),
            pl.BlockSpec((d, tn), lambda j: (0, j)),
            pl.BlockSpec((1, tn), lambda j: (0, j)),
        ],
        out_specs=pl.BlockSpec((8, tn), lambda j: (0, j)),
        compiler_params=_cparams(("arbitrary",)),
        name="ada_modulation",
    )(cb, ada_w, ada_b.reshape(1, n))


def _rms_mod(xf, g, shift, scale):
    var = jnp.mean(xf * xf, axis=-1, keepdims=True)
    y = xf * lax.rsqrt(var + EPS) * g
    return y * (1.0 + scale) + shift


def _norm_mod_kernel(x_ref, ctx_ref, ss_ref, g_ref, o_ref, *, nc):
    i = pl.program_id(1)

    def emit(src):
        h = _rms_mod(src[0], g_ref[...], ss_ref[0, 0:1, :], ss_ref[0, 1:2, :])
        o_ref[0] = h.astype(o_ref.dtype)

    @pl.when(i < nc)
    def _():
        emit(ctx_ref)

    @pl.when(i >= nc)
    def _():
        emit(x_ref)


def _norm_mod_all(x, ctx, ss, g):
    b, t, d = x.shape
    tc = ctx.shape[1]
    tm = ROW_TILE
    nc = tc // tm
    r = tc + t
    return pl.pallas_call(
        functools.partial(_norm_mod_kernel, nc=nc),
        out_shape=jax.ShapeDtypeStruct((b, r, d), BF16),
        grid=(b, r // tm),
        in_specs=[
            pl.BlockSpec((1, tm, d), lambda bi, i: (bi, jnp.maximum(i - nc, 0), 0)),
            pl.BlockSpec((1, tm, d), lambda bi, i: (bi, jnp.minimum(i, nc - 1), 0)),
            pl.BlockSpec((1, 2, d), lambda bi, i: (jnp.where(i < nc, b, bi), 0, 0)),
            pl.BlockSpec((1, d), lambda bi, i: (0, 0)),
        ],
        out_specs=pl.BlockSpec((1, tm, d), lambda bi, i: (bi, i, 0)),
        compiler_params=_cparams(("arbitrary", "arbitrary")),
        name="norm_mod",
    )(x, ctx, ss, g)


def _matmul_kernel(a_ref, b_ref, o_ref):
    o_ref[...] = _dot(a_ref[...], b_ref[...]).astype(o_ref.dtype)


def _matmul(a, b, tm, tn, out_dtype, name):
    m, k = a.shape
    n = b.shape[1]
    return pl.pallas_call(
        _matmul_kernel,
        out_shape=jax.ShapeDtypeStruct((m, n), out_dtype),
        grid=(n // tn, m // tm),
        in_specs=[
            pl.BlockSpec((tm, k), lambda j, i: (i, 0)),
            pl.BlockSpec((k, tn), lambda j, i: (0, j)),
        ],
        out_specs=pl.BlockSpec((tm, tn), lambda j, i: (i, j)),
        compiler_params=_cparams(("arbitrary", "arbitrary")),
        name=name,
    )(a, b)


def _rope_tables(t_lat, t_ctx):
    rows = t_lat // GRID_W
    row = np.repeat(np.arange(rows), GRID_W).astype(np.float64)
    col = np.tile(np.arange(GRID_W), rows).astype(np.float64)
    half = DA_DH // 2
    inv = ROPE_BASE ** (-np.arange(0, half, 2, dtype=np.float64) / half)
    ar = row[:, None] * inv
    ac = col[:, None] * inv
    cr, sr, cc, sc = np.cos(ar), np.sin(ar), np.cos(ac), np.sin(ac)
    z = np.zeros_like(sr)
    cos64 = np.concatenate([cr, cr, cc, cc], axis=1)
    sa64 = np.concatenate([-sr, z, -sc, z], axis=1)
    sb64 = np.concatenate([z, sr, z, sc], axis=1)
    tile2 = lambda a: np.concatenate([a, a], axis=1)
    pad = lambda a, v: jnp.asarray(
        np.concatenate([np.full((t_ctx, LANES), v), tile2(a)], axis=0), F32)
    return pad(cos64, 1.0), pad(sa64, 0.0), pad(sb64, 0.0)


def _qk_prep_kernel(x_ref, cos_ref, sa_ref, sb_ref, g_ref, bd_ref, o_ref, *, is_q):
    lane = lax.broadcasted_iota(jnp.int32, cos_ref.shape, 1)
    for h in range(DA_HEADS):
        x = x_ref[0, :, h * LANES:(h + 1) * LANES].astype(F32)
        ss = _exact_lhs_dot_rhs(x * x, bd_ref[...])
        y = x * lax.rsqrt(ss * (1.0 / DA_DH) + EPS) * g_ref[...]
        up = pltpu.roll(y, LANES - 16, axis=1)
        dn = pltpu.roll(y, 16, axis=1)
        o = y * cos_ref[...] + up * sa_ref[...] + dn * sb_ref[...]
        if is_q:
            o = o * (DA_DH ** -0.5 * LOG2_E)
            zero = jnp.zeros_like(o)
            o_ref[0, :, 2 * h * LANES:(2 * h + 1) * LANES] = jnp.where(lane < DA_DH, o, zero).astype(o_ref.dtype)
            o_ref[0, :, (2 * h + 1) * LANES:(2 * h + 2) * LANES] = jnp.where(lane >= DA_DH, o, zero).astype(o_ref.dtype)
        else:
            o_ref[0, :, h * LANES:(h + 1) * LANES] = o.astype(o_ref.dtype)


def _exact_lhs_dot_rhs(x, m_bf16):
    p1, p2, p3 = _split3(x)
    return _dot(p1, m_bf16) + _dot(p2, m_bf16) + _dot(p3, m_bf16)


def _qk_prep(proj3, cos_t, sa_t, sb_t, gain, col0, is_q, t_ctx):
    b, r, _ = proj3.shape
    tm = ROW_TILE
    row0 = t_ctx // tm if is_q else 0
    n_rows = r - t_ctx if is_q else r
    width = DA_HEADS * (2 * LANES if is_q else LANES)
    bd = jnp.asarray(np.kron(np.eye(2), np.ones((DA_DH, DA_DH))), BF16)
    g2 = jnp.concatenate([gain, gain]).reshape(1, LANES).astype(F32)
    return pl.pallas_call(
        functools.partial(_qk_prep_kernel, is_q=is_q),
        out_shape=jax.ShapeDtypeStruct((b, n_rows, width), BF16),
        grid=(b, n_rows // tm),
        in_specs=[
            pl.BlockSpec((1, tm, DA_HEADS * LANES), lambda bi, i: (bi, row0 + i, col0 // DA_HEADS)),
            pl.BlockSpec((tm, LANES), lambda bi, i: (row0 + i, 0)),
            pl.BlockSpec((tm, LANES), lambda bi, i: (row0 + i, 0)),
            pl.BlockSpec((tm, LANES), lambda bi, i: (row0 + i, 0)),
            pl.BlockSpec((1, LANES), lambda bi, i: (0, 0)),
            pl.BlockSpec((LANES, LANES), lambda bi, i: (0, 0)),
        ],
        out_specs=pl.BlockSpec((1, tm, width), lambda bi, i: (bi, i, 0)),
        compiler_params=_cparams(("arbitrary", "arbitrary")),
        name="q_prep" if is_q else "k_prep",
    )(proj3, cos_t, sa_t, sb_t, g2, bd)


def _attn_body(q_ref, k_ref, v_ref, lam_ref, g_ref, o_ref, sp_ref, mp_ref, sc_ref, mc_ref, *, tk, nk):
    tq = q_ref.shape[1]
    q = q_ref[0]
    qs = jnp.concatenate([q[:, :LANES], q[:, LANES:]], axis=0)
    m_c = mc_ref[:, 0:1]
    ones = jnp.ones((tk, LANES), BF16)
    acc = jnp.zeros((2 * tq, 2 * LANES), F32)
    mpart = None
    for c in range(nk):
        sl = slice(c * tk, (c + 1) * tk)
        s = _dot_nt(qs, k_ref[0, sl, :])
        sp_ref[:, sl] = s
        for j in range(tk // LANES):
            blk = s[:, j * LANES:(j + 1) * LANES]
            mpart = blk if mpart is None else jnp.maximum(mpart, blk)
        p = jnp.exp2((sc_ref[:, sl] - m_c).astype(BF16))
        vaug = jnp.concatenate([v_ref[0, sl, :], ones], axis=1)
        acc = acc + _dot(p, vaug)
    mp_ref[...] = jnp.broadcast_to(jnp.max(mpart, axis=1, keepdims=True), mp_ref.shape)
    ratio = acc[:, :LANES] / acc[:, LANES:]
    lam = (jnp.exp(jnp.sum(lam_ref[0:1, :] * lam_ref[1:2, :], axis=1, keepdims=True))
           - jnp.exp(jnp.sum(lam_ref[2:3, :] * lam_ref[3:4, :], axis=1, keepdims=True)) + LAM_INIT)
    o = ratio[:tq] - lam * ratio[tq:]
    var = jnp.mean(o * o, axis=-1, keepdims=True)
    o = o * lax.rsqrt(var + EPS) * g_ref[...] * (1.0 - LAM_INIT)
    o_ref[0] = o.astype(o_ref.dtype)


def _attn_kernel(q_ref, k_ref, v_ref, lam_ref, g_ref, o_ref, sa_ref, ma_ref, sb_ref, mb_ref, *, tk, nk):
    f = pl.program_id(0)
    body = functools.partial(_attn_body, q_ref, k_ref, v_ref, lam_ref, g_ref, o_ref, tk=tk, nk=nk)

    @pl.when(f == 0)
    def _():
        sb_ref[...] = jnp.zeros(sb_ref.shape, F32)
        mb_ref[...] = jnp.zeros(mb_ref.shape, F32)

    @pl.when(lax.rem(f, 2) == 0)
    def _():
        body(sa_ref, ma_ref, sb_ref, mb_ref)

    @pl.when(lax.rem(f, 2) == 1)
    def _():
        body(sb_ref, mb_ref, sa_ref, ma_ref)


def _diff_attention(qm, kh, proj3, v_col0, lam_rows, subln_g):
    b, t, _ = qm.shape
    r = kh.shape[1]
    tq = ROW_TILE
    tk = ROW_TILE
    n_q = t // tq
    total = b * DA_HEADS * n_q

    def split(idx):
        return idx // (DA_HEADS * n_q), lax.rem(idx // n_q, DA_HEADS), lax.rem(idx, n_q)

    prod = lambda f: split(jnp.minimum(f, total - 1))
    cons = lambda f: split(jnp.maximum(f - 1, 0))
    return pl.pallas_call(
        functools.partial(_attn_kernel, tk=tk, nk=r // tk),
        out_shape=jax.ShapeDtypeStruct((b, t, DA_HEADS * DA_DV), BF16),
        grid=(total + 1,),
        in_specs=[
            pl.BlockSpec((1, tq, 2 * LANES), lambda f: (prod(f)[0], prod(f)[2], prod(f)[1])),
            pl.BlockSpec((1, r, LANES), lambda f: (prod(f)[0], 0, prod(f)[1])),
            pl.BlockSpec((1, r, LANES), lambda f: (cons(f)[0], 0, v_col0 + cons(f)[1])),
            pl.BlockSpec((8, LANES), lambda f: (0, 0)),
            pl.BlockSpec((1, LANES), lambda f: (0, 0)),
        ],
        out_specs=pl.BlockSpec((1, tq, DA_DV), lambda f: (cons(f)[0], cons(f)[2], cons(f)[1])),
        scratch_shapes=[pltpu.VMEM((2 * tq, r), F32), pltpu.VMEM((2 * tq, LANES), F32),
                        pltpu.VMEM((2 * tq, r), F32), pltpu.VMEM((2 * tq, LANES), F32)],
        compiler_params=_cparams(("arbitrary",)),
        name="diff_attention",
    )(qm, kh, proj3, lam_rows, subln_g.reshape(1, DA_DV).astype(F32))


def _log_sigmoid(z):
    return jnp.minimum(z, 0.0) - jnp.log(1.0 + jnp.exp(-jnp.abs(z)))


def _gla_kernel(*refs, reverse, final, n_chunks):
    if final:
        (q_ref, k_ref, v_ref, lr_ref, w2_ref, b_ref, tri_ref, ones_ref,
         of_ref, gr_ref, ng_ref, o_ref, st_ref) = refs
    else:
        (q_ref, k_ref, v_ref, lr_ref, w2_ref, b_ref, tri_ref, ones_ref, o_ref, st_ref) = refs
    c_len = GLA_CHUNK

    @pl.when(pl.program_id(2) == 0)
    def _():
        st_ref[...] = jnp.zeros(st_ref.shape, F32)

    q = q_ref[0].astype(F32) * (GLA_DK ** -0.5)
    k = k_ref[0].astype(F32)
    v = v_ref[0]
    z = _dot3(lr_ref[0], w2_ref[0]) + b_ref[0]
    la = _log_sigmoid(z) * (1.0 / GLA_TAU)
    tri = tri_ref[...]
    cum = _exact_lhs_dot(tri, la)
    tot = _exact_lhs_dot(ones_ref[...], la)
    q_in = (q * jnp.exp(cum)).astype(BF16)
    k_in = (k * jnp.exp(-cum)).astype(BF16)
    k_st = (k * jnp.exp(tot - cum)).astype(BF16)
    dec = jnp.exp(tot)
    att = jnp.where(tri > 0, _dot_nt(q_in, k_in), 0.0)
    o_intra = _dot(att.astype(BF16), v)

    order = range(n_chunks - 1, -1, -1) if reverse else range(n_chunks)
    outs = [None] * n_chunks
    for c in order:
        sl = slice(c * c_len, (c + 1) * c_len)
        st = st_ref[...]
        outs[c] = o_intra[sl] + _dot_nt(q_in[sl], st.astype(BF16))
        kv = _dot_tn(v[sl], k_st[sl])
        st_ref[...] = st * dec[c * c_len:c * c_len + 1, :] + kv
    o = jnp.concatenate(outs, axis=0)
    if final:
        o = o + of_ref[0].astype(F32)
        var = jnp.mean(o * o, axis=-1, keepdims=True)
        y = o * lax.rsqrt(var + EPS) * ng_ref[...]
        gr = gr_ref[0].astype(F32)
        y = y * (gr * (1.0 / (1.0 + jnp.exp(-gr))))
        o_ref[0] = y.astype(o_ref.dtype)
    else:
        o_ref[0] = o.astype(o_ref.dtype)


def _gla_direction(proj3, lr3, w2p, bias, t_ctx, cols, reverse, o_fwd=None, norm_g=None):
    b, r, _ = proj3.shape
    tc = ROW_TILE
    n_chunks = tc // GLA_CHUNK
    ng = r // tc
    ncg = t_ctx // tc
    final = o_fwd is not None
    q_col, k_col, v_col, gr_col = cols

    blk = np.kron(np.eye(n_chunks), np.ones((GLA_CHUNK, GLA_CHUNK)))
    full_tri = np.tril(np.ones((tc, tc)))
    tri = blk * (full_tri.T if reverse else full_tri)
    tri = jnp.asarray(tri, BF16)
    ones_bd = jnp.asarray(blk, BF16)

    if reverse:
        def grp(s):
            return jnp.where(s < ncg, ncg - 1 - s, ng - 1 - (s - ncg))
    else:
        def grp(s):
            return s

    def lat(s):
        return jnp.maximum(grp(jnp.maximum(s, ncg)) - ncg, 0)

    in_specs = [
        pl.BlockSpec((1, tc, GLA_DK), lambda bi, h, s: (bi, grp(s), q_col + h)),
        pl.BlockSpec((1, tc, GLA_DK), lambda bi, h, s: (bi, grp(s), k_col + h)),
        pl.BlockSpec((1, tc, GLA_DV), lambda bi, h, s: (bi, grp(s), v_col + h)),
        pl.BlockSpec((1, tc, LANES), lambda bi, h, s: (bi, grp(s), 0)),
        pl.BlockSpec((1, LANES, GLA_DK), lambda bi, h, s: (h, 0, 0)),
        pl.BlockSpec((1, 1, GLA_DK), lambda bi, h, s: (h, 0, 0)),
        pl.BlockSpec((tc, tc), lambda bi, h, s: (0, 0)),
        pl.BlockSpec((tc, tc), lambda bi, h, s: (0, 0)),
    ]
    args = [proj3, proj3, proj3, lr3, w2p, bias, tri, ones_bd]
    if final:
        in_specs += [
            pl.BlockSpec((1, tc, GLA_DV), lambda bi, h, s: (bi, lat(s), h)),
            pl.BlockSpec((1, tc, GLA_DV), lambda bi, h, s: (bi, grp(s), gr_col + h)),
            pl.BlockSpec((1, GLA_DV), lambda bi, h, s: (0, 0)),
        ]
        args += [o_fwd, proj3, norm_g.reshape(1, GLA_DV).astype(F32)]
    return pl.pallas_call(
        functools.partial(_gla_kernel, reverse=reverse, final=final, n_chunks=n_chunks),
        out_shape=jax.ShapeDtypeStruct((b, r - t_ctx, GLA_HEADS * GLA_DV), BF16 if final else F32),
        grid=(b, GLA_HEADS, ng),
        in_specs=in_specs,
        out_specs=pl.BlockSpec((1, tc, GLA_DV), lambda bi, h, s: (bi, lat(s), h)),
        scratch_shapes=[pltpu.VMEM((GLA_DV, GLA_DK), F32)],
        compiler_params=_cparams(("arbitrary", "arbitrary", "arbitrary")),
        name="gla_bwd" if reverse else "gla_fwd",
    )(*args)


def _sigmoid(x):
    return 1.0 / (1.0 + jnp.exp(-x))


def _merge_kernel(ya_ref, yb_ref, wa_ref, wb_ref, ga_ref, gb_ref, o_ref):
    a = _dot(ya_ref[0], wa_ref[...])
    bb = _dot(yb_ref[0], wb_ref[...])
    z = _sigmoid(ga_ref[0].astype(F32)) * a + _sigmoid(gb_ref[0].astype(F32)) * bb
    o_ref[0] = z.astype(o_ref.dtype)


def _merge(y_a, y_b, wa, wb, proj3, ga_col0, gb_col0, t_ctx):
    b, t, ka = y_a.shape
    kb = y_b.shape[2]
    d = wa.shape[1]
    tm, tn = ROW_TILE, 1024
    row0 = t_ctx // tm
    return pl.pallas_call(
        _merge_kernel,
        out_shape=jax.ShapeDtypeStruct((b, t, d), BF16),
        grid=(d // tn, b, t // tm),
        in_specs=[
            pl.BlockSpec((1, tm, ka), lambda j, bi, i: (bi, i, 0)),
            pl.BlockSpec((1, tm, kb), lambda j, bi, i: (bi, i, 0)),
            pl.BlockSpec((ka, tn), lambda j, bi, i: (0, j)),
            pl.BlockSpec((kb, tn), lambda j, bi, i: (0, j)),
            pl.BlockSpec((1, tm, tn), lambda j, bi, i: (bi, row0 + i, ga_col0 // tn + j)),
            pl.BlockSpec((1, tm, tn), lambda j, bi, i: (bi, row0 + i, gb_col0 // tn + j)),
        ],
        out_specs=pl.BlockSpec((1, tm, tn), lambda j, bi, i: (bi, i, j)),
        compiler_params=_cparams(("arbitrary", "arbitrary", "arbitrary")),
        name="merge",
    )(y_a, y_b, wa, wb, proj3, proj3)


def _out_proj_kernel(z_ref, w_ref, x_ref, g_ref, o_ref):
    y = _dot(z_ref[0], w_ref[...])
    o_ref[0] = x_ref[0] + g_ref[0] * y


def _out_proj_residual(z, w, x, gate):
    b, t, d = x.shape
    tm, tn = 512, 1024
    return pl.pallas_call(
        _out_proj_kernel,
        out_shape=jax.ShapeDtypeStruct((b, t, d), F32),
        grid=(d // tn, b, t // tm),
        in_specs=[
            pl.BlockSpec((1, tm, d), lambda j, bi, i: (bi, i, 0)),
            pl.BlockSpec((d, tn), lambda j, bi, i: (0, j)),
            pl.BlockSpec((1, tm, tn), lambda j, bi, i: (bi, i, j)),
            pl.BlockSpec((1, 1, tn), lambda j, bi, i: (bi, 0, j)),
        ],
        out_specs=pl.BlockSpec((1, tm, tn), lambda j, bi, i: (bi, i, j)),
        compiler_params=_cparams(("arbitrary", "arbitrary", "arbitrary")),
        name="out_proj",
    )(z, w, x, gate)


def _peer_q_kernel(x_ref, ss_ref, g_ref, w_ref, h_ref, q_ref):
    h = _rms_mod(x_ref[0], g_ref[...], ss_ref[0, 0:1, :], ss_ref[0, 1:2, :]).astype(BF16)
    h_ref[0] = h
    q_ref[0] = _dot(h, w_ref[...])


def _peer_query(x1, ss, g, wq):
    b, t, d = x1.shape
    n = wq.shape[1]
    tm = ROW_TILE
    return pl.pallas_call(
        _peer_q_kernel,
        out_shape=(jax.ShapeDtypeStruct((b, t, d), BF16), jax.ShapeDtypeStruct((b, t, n), F32)),
        grid=(b, t // tm),
        in_specs=[
            pl.BlockSpec((1, tm, d), lambda bi, i: (bi, i, 0)),
            pl.BlockSpec((1, 2, d), lambda bi, i: (bi, 0, 0)),
            pl.BlockSpec((1, d), lambda bi, i: (0, 0)),
            pl.BlockSpec((d, n), lambda bi, i: (0, 0)),
        ],
        out_specs=(pl.BlockSpec((1, tm, d), lambda bi, i: (bi, i, 0)),
                   pl.BlockSpec((1, tm, n), lambda bi, i: (bi, i, 0))),
        compiler_params=_cparams(("arbitrary", "arbitrary")),
        name="peer_query",
    )(x1, ss, g, wq)


SUBLANES = 8


def _cmp_exchange(xs, hi, lo):
    a, b = xs[hi], xs[lo]
    if a is None and b is None:
        return
    if a is None or b is None:
        xs[hi], xs[lo] = (b if a is None else a), None
        return
    xs[hi], xs[lo] = jnp.maximum(a, b), jnp.minimum(a, b)


def _bitonic_sort_desc(xs):
    n = len(xs)
    xs = list(xs)
    k = 2
    while k <= n:
        j = k // 2
        while j >= 1:
            for i in range(n):
                l = i ^ j
                if l > i:
                    if (i & k) == 0:
                        _cmp_exchange(xs, i, l)
                    else:
                        _cmp_exchange(xs, l, i)
            j //= 2
        k *= 2
    return xs


def _bitonic_merge_desc(xs):
    n = len(xs)
    xs = list(xs)
    j = n // 2
    while j >= 1:
        for i in range(n):
            l = i ^ j
            if l > i:
                _cmp_exchange(xs, i, l)
        j //= 2
    return xs


def _top16_sorted(xs):
    n = PEER_TOPK
    xs = list(xs) + [None] * (n - len(xs))
    xs = _bitonic_sort_desc(xs)
    shift = SUBLANES // 2
    while shift >= 1:
        merged = []
        for i in range(n):
            a, b = xs[i], xs[n - 1 - i]
            if b is not None:
                b = pltpu.roll(b, shift, axis=0)
            merged.append(b if a is None else (a if b is None else jnp.maximum(a, b)))
        xs = _bitonic_merge_desc(merged)
        shift //= 2
    return xs


def _route_kernel(q_ref, keys_ref, s1_ref, s2_ref, tau_ref):
    q = q_ref[0]
    nk = PEER_NKEYS
    s1 = _dot3_nt(keys_ref[0, 0], q[:, :nk]) * LOG2_E
    s2 = _dot3_nt(keys_ref[0, 1], q[:, nk:]) * LOG2_E
    split = lambda s: [s[i * SUBLANES:(i + 1) * SUBLANES] for i in range(nk // SUBLANES)]
    v1 = _top16_sorted(split(s1))
    v2 = _top16_sorted(split(s2))
    sub = lax.broadcasted_iota(jnp.int32, v1[0].shape, 0)

    def spread(vals):
        out = vals[0]
        for j in range(1, SUBLANES):
            out = jnp.where(sub == j, vals[j], out)
        return out

    v2_lo, v2_hi = spread(v2[:SUBLANES]), spread(v2[SUBLANES:])

    def top_pairs(w1):
        cands = [w1[0] + v2_lo, w1[0] + v2_hi]
        cands += [w1[i] + v2_lo for i in range(1, SUBLANES)]
        cands += [spread(w1[SUBLANES:]) + v2[0]]
        return _top16_sorted(cands)

    top = top_pairs(v1)
    m = top[0]
    zsum = jnp.zeros_like(m)
    for c in top:
        zsum = zsum + jnp.exp2(c - m)
    off = m + jnp.log2(zsum)
    tau = top_pairs([v - off for v in v1])[PEER_TOPK - 1]
    s1_ref[0, 0] = s1 - off[0:1]
    s2_ref[0, 0] = s2
    tau_ref[0, 0] = tau[0:1]


def _peer_route(q, keys):
    b, t, _ = q.shape
    tt = 512
    nk = PEER_NKEYS
    big = jax.ShapeDtypeStruct((b, PEER_HEADS, nk, t), F32)
    spec = pl.BlockSpec((1, 1, nk, tt), lambda bi, i, h: (bi, h, 0, i))
    return pl.pallas_call(
        _route_kernel,
        out_shape=(big, big, jax.ShapeDtypeStruct((b, PEER_HEADS, 1, t), F32)),
        grid=(b, t // tt, PEER_HEADS),
        in_specs=[
            pl.BlockSpec((1, tt, 2 * nk), lambda bi, i, h: (bi, i, h)),
            pl.BlockSpec((1, 2, nk, nk), lambda bi, i, h: (h, 0, 0, 0)),
        ],
        out_specs=(spec, spec,
                   pl.BlockSpec((1, 1, 1, tt), lambda bi, i, h: (bi, h, 0, i))),
        compiler_params=_cparams(("arbitrary", "arbitrary", "arbitrary")),
        name="peer_route",
    )(q, keys)


def _gelu(x):
    return 0.5 * x * (1.0 + lax.erf(x * (2.0 ** -0.5)))


def _peer_mix_kernel(h_ref, u_ref, vt_ref, s1_ref, s2_ref, tau_ref,
                     o_ref, st_ref, gt_ref, *, n_a, n_e):
    f = pl.program_id(0)
    nk = PEER_NKEYS
    slot_p = lax.rem(f, 2)
    slot_c = 1 - slot_p
    e_c = lax.rem(jnp.maximum(f - 1, 0), n_e)

    @pl.when(f == 0)
    def _():
        st_ref[1] = jnp.zeros(st_ref.shape[1:], F32)
        gt_ref[1] = jnp.zeros(gt_ref.shape[1:], F32)

    @pl.when(e_c == 0)
    def _():
        o_ref[...] = jnp.zeros(o_ref.shape, F32)

    a_t = (_gelu(st_ref[slot_c]) * gt_ref[slot_c]).astype(BF16)
    o_ref[0] += _dot(vt_ref[...], a_t)

    st_ref[slot_p] = _dot_nt(u_ref[...], h_ref[0])
    for al in range(n_a):
        gate = None
        for h in range(PEER_HEADS):
            cand = s1_ref[0, h, al:al + 1, :] + s2_ref[0, h]
            term = jnp.where(cand >= tau_ref[0, h], jnp.exp2(cand), 0.0)
            gate = term if gate is None else gate + term
        gt_ref[slot_p, al * nk:(al + 1) * nk, :] = gate


def _peer_mix(h2, u_bf, vt_bf, s1, s2, tau):
    b, t, d = h2.shape
    nk = PEER_NKEYS
    n_exp = u_bf.shape[0]
    tt = 512
    n_a = 8
    eb = n_a * nk
    n_e = n_exp // eb
    n_t = t // tt
    total = b * n_t * n_e

    def split(idx):
        return idx // (n_t * n_e), lax.rem(idx // n_e, n_t), lax.rem(idx, n_e)

    prod = lambda f: split(jnp.minimum(f, total - 1))
    cons = lambda f: split(jnp.maximum(f - 1, 0))

    def tok_spec(rows):
        def imap(f):
            bi, ti, _ = prod(f)
            return (bi, 0, 0, ti)
        return pl.BlockSpec((1, PEER_HEADS, rows, tt), imap)

    def blk_spec():
        def imap(f):
            bi, ti, e = prod(f)
            return (bi, 0, e, ti)
        return pl.BlockSpec((1, PEER_HEADS, n_a, tt), imap)

    return pl.pallas_call(
        functools.partial(_peer_mix_kernel, n_a=n_a, n_e=n_e),
        out_shape=jax.ShapeDtypeStruct((b, d, t), F32),
        grid=(total + 1,),
        in_specs=[
            pl.BlockSpec((1, tt, d), lambda f: (prod(f)[0], prod(f)[1], 0)),
            pl.BlockSpec((eb, d), lambda f: (prod(f)[2], 0)),
            pl.BlockSpec((d, eb), lambda f: (0, cons(f)[2])),
            blk_spec(), tok_spec(nk), tok_spec(1),
        ],
        out_specs=pl.BlockSpec((1, d, tt), lambda f: (cons(f)[0], 0, cons(f)[1])),
        scratch_shapes=[pltpu.VMEM((2, eb, tt), F32), pltpu.VMEM((2, eb, tt), F32)],
        compiler_params=_cparams(("arbitrary",)),
        name="peer_mix",
    )(h2, u_bf, vt_bf, s1, s2, tau)


def _peer_out_kernel(acc_ref, x_ref, g_ref, o_ref):
    o_ref[0] = x_ref[0] + g_ref[0] * acc_ref[0].T


def _peer_residual(acc_t, x1, gate):
    b, t, d = x1.shape
    tt = 512
    return pl.pallas_call(
        _peer_out_kernel,
        out_shape=jax.ShapeDtypeStruct((b, t, d), F32),
        grid=(b, t // tt),
        in_specs=[
            pl.BlockSpec((1, d, tt), lambda bi, i: (bi, 0, i)),
            pl.BlockSpec((1, tt, d), lambda bi, i: (bi, i, 0)),
            pl.BlockSpec((1, 1, d), lambda bi, i: (bi, 0, 0)),
        ],
        out_specs=pl.BlockSpec((1, tt, d), lambda bi, i: (bi, i, 0)),
        compiler_params=_cparams(("arbitrary", "arbitrary")),
        name="peer_residual",
    )(acc_t, x1, gate)


def kernel(x, c, ctx, c_ctx, ada_w, ada_b, norm1_g, norm2_g, w_in, da_qn_g, da_kn_g, da_lam_q1, da_lam_k1, da_lam_q2, da_lam_k2, da_subln_g, gla_w2_f, gla_b_f, gla_w2_b, gla_b_b, gla_norm_g, w_br_a, w_br_b, w_out, peer_wq, peer_keys, peer_u, peer_v):
    b, t, d = x.shape
    t_ctx = ctx.shape[1]
    assert ada_w.shape[0] == 1, "single-layer kernel"
    assert t % ROW_TILE == 0 and t_ctx % ROW_TILE == 0 and t % GRID_W == 0
    l = 0

    cond = jnp.concatenate([c, c_ctx[None, :]], axis=0)
    mod = _ada_modulation(cond, ada_w[l], ada_b[l])[:b + 1].reshape(b + 1, 6, d)
    ss1 = mod[:, 0:2]
    g1 = mod[:b, 2:3]
    ss2 = mod[:b, 3:5]
    g2 = mod[:b, 5:6]

    h_all = _norm_mod_all(x, ctx, ss1, norm1_g[l].reshape(1, d))
    r = t_ctx + t
    wi = w_in[l]
    o_glr = 6144
    w_main = jnp.concatenate([wi[:, :o_glr], wi[:, o_glr + 2 * GLA_RANK:]], axis=1).astype(BF16)
    w_lr = jnp.pad(wi[:, o_glr:o_glr + 2 * GLA_RANK], ((0, 0), (0, LANES - 2 * GLA_RANK))).astype(BF16)
    h2d = h_all.reshape(b * r, d)
    n_main = w_main.shape[1]
    proj3 = _matmul(h2d, w_main, 512, 2048, BF16, "in_proj").reshape(b, r, n_main)
    lr3 = _matmul(h2d, w_lr, 512, LANES, F32, "in_proj_lr").reshape(b, r, LANES)

    cos_t, sa_t, sb_t = _rope_tables(t, t_ctx)
    qm = _qk_prep(proj3, cos_t, sa_t, sb_t, da_qn_g[l], 0, True, t_ctx)
    kh = _qk_prep(proj3, cos_t, sa_t, sb_t, da_kn_g[l], 8, False, t_ctx)
    lam_rows = jnp.zeros((8, LANES), F32)
    lam_rows = lam_rows.at[0, :DA_DH].set(da_lam_q1[l]).at[1, :DA_DH].set(da_lam_k1[l])
    lam_rows = lam_rows.at[2, :DA_DH].set(da_lam_q2[l]).at[3, :DA_DH].set(da_lam_k2[l])
    y_a = _diff_attention(qm, kh, proj3, 16, lam_rows, da_subln_g[l])

    def pad_w2(w2, row0):
        w = w2.reshape(GLA_RANK, GLA_HEADS, GLA_DK).transpose(1, 0, 2)
        return jnp.pad(w, ((0, 0), (row0, LANES - GLA_RANK - row0), (0, 0))).astype(F32)

    cols = (24, 28, 16, 20)
    o_f = _gla_direction(proj3, lr3, pad_w2(gla_w2_f[l], 0),
                         gla_b_f[l].reshape(GLA_HEADS, 1, GLA_DK), t_ctx, cols, False)
    y_b = _gla_direction(proj3, lr3, pad_w2(gla_w2_b[l], GLA_RANK),
                         gla_b_b[l].reshape(GLA_HEADS, 1, GLA_DK), t_ctx, cols, True,
                         o_fwd=o_f, norm_g=gla_norm_g[l])

    z = _merge(y_a, y_b, w_br_a[l].astype(BF16), w_br_b[l].astype(BF16), proj3, 6144, 8192, t_ctx)
    x1 = _out_proj_residual(z, w_out[l].astype(BF16), x, g1)

    h2, q = _peer_query(x1, ss2, norm2_g[l].reshape(1, d), peer_wq[l].astype(BF16))
    s1, s2, tau = _peer_route(q, peer_keys[l])
    acc_t = _peer_mix(h2, peer_u[l].astype(BF16), peer_v[l].T.astype(BF16), s1, s2, tau)
    return _peer_residual(acc_t, x1, g2)
```

```python
import functools
import math

import jax
import jax.numpy as jnp
import numpy as np
from jax import lax
from jax.experimental import pallas as pl
from jax.experimental.pallas import tpu as pltpu

F32 = jnp.float32
BF16 = jnp.bfloat16

GRID_W = 64
DA_HEADS = 8
DA_DH = 64
DA_DV = 128
GLA_HEADS = 4
GLA_DK = 128
GLA_DV = 256
GLA_RANK = 16
GLA_TAU = 16.0
GLA_CHUNK = 64
PEER_HEADS = 8
PEER_NKEYS = 128
PEER_TOPK = 16
ROPE_BASE = 10000.0
EPS = 1e-6
LAM_INIT = 0.8 - 0.6 * math.exp(-0.3 * 0)

LANES = 128
ROW_TILE = 256
VMEM_LIMIT = 56 * 1024 * 1024

NEG_INF = float("-inf")
LOG2_E = 1.0 / math.log(2.0)


def _cparams(sem):
    return pltpu.CompilerParams(dimension_semantics=sem, vmem_limit_bytes=VMEM_LIMIT)


def _split2(x):
    hi = x.astype(BF16)
    lo = (x - hi.astype(F32)).astype(BF16)
    return hi, lo


def _split3(x):
    p1 = x.astype(BF16)
    r1 = x - p1.astype(F32)
    p2 = r1.astype(BF16)
    p3 = (r1 - p2.astype(F32)).astype(BF16)
    return p1, p2, p3


def _dot(a, b):
    return jnp.dot(a, b, preferred_element_type=F32)


def _dot_nt(a, b):
    return lax.dot_general(a, b, (((1,), (1,)), ((), ())), preferred_element_type=F32)


def _dot_tn(a, b):
    return lax.dot_general(a, b, (((0,), (0,)), ((), ())), preferred_element_type=F32)


def _dot3_nt(a, b):
    ah, al = _split2(a)
    bh, bl = _split2(b)
    return _dot_nt(ah, bh) + _dot_nt(ah, bl) + _dot_nt(al, bh)


def _dot3(a, b):
    ah, al = _split2(a)
    bh, bl = _split2(b)
    return _dot(ah, bh) + _dot(ah, bl) + _dot(al, bh)


def _exact_lhs_dot(m_bf16, x):
    p1, p2, p3 = _split3(x)
    return _dot(m_bf16, p1) + _dot(m_bf16, p2) + _dot(m_bf16, p3)


def _ada_kernel(cb_ref, w_ref, b_ref, o_ref, *, n_rows):
    tn = w_ref.shape[1]
    o_ref[...] = jnp.zeros(o_ref.shape, F32)
    for r in range(n_rows):
        cv = cb_ref[r]
        s = cv * (1.0 / (1.0 + jnp.exp(-cv)))
        for j in range(tn // LANES):
            sl = slice(j * LANES, (j + 1) * LANES)
            col = jnp.sum(s * w_ref[:, sl], axis=0, keepdims=True)
            o_ref[r:r + 1, sl] = col + b_ref[:, sl]


def _ada_modulation(cond, ada_w, ada_b):
    n_rows, d = cond.shape
    n = ada_w.shape[1]
    tn = 1024
    cb = jnp.broadcast_to(cond[:, :, None], (n_rows, d, LANES))
    return pl.pallas_call(
        functools.partial(_ada_kernel, n_rows=n_rows),
        out_shape=jax.ShapeDtypeStruct((8, n), F32),
        grid=(n // tn,),
        in_specs=[
            pl.BlockSpec((n_rows, d, LANES), lambda j: (0, 0, 0)),
            pl.BlockSpec((d, tn), lambda j: (0, j)),
            pl.BlockSpec((1, tn), lambda j: (0, j)),
        ],
        out_specs=pl.BlockSpec((8, tn), lambda j: (0, j)),
        compiler_params=_cparams(("arbitrary",)),
        name="ada_modulation",
    )(cb, ada_w, ada_b.reshape(1, n))


def _rms_mod(xf, g, shift, scale):
    var = jnp.mean(xf * xf, axis=-1, keepdims=True)
    y = xf * lax.rsqrt(var + EPS) * g
    return y * (1.0 + scale) + shift


def _norm_mod_kernel(x_ref, ctx_ref, ss_ref, g_ref, o_ref, *, nc):
    i = pl.program_id(1)

    def emit(src):
        h = _rms_mod(src[0], g_ref[...], ss_ref[0, 0:1, :], ss_ref[0, 1:2, :])
        o_ref[0] = h.astype(o_ref.dtype)

    @pl.when(i < nc)
    def _():
        emit(ctx_ref)

    @pl.when(i >= nc)
    def _():
        emit(x_ref)


def _norm_mod_all(x, ctx, ss, g):
    b, t, d = x.shape
    tc = ctx.shape[1]
    tm = ROW_TILE
    nc = tc // tm
    r = tc + t
    return pl.pallas_call(
        functools.partial(_norm_mod_kernel, nc=nc),
        out_shape=jax.ShapeDtypeStruct((b, r, d), BF16),
        grid=(b, r // tm),
        in_specs=[
            pl.BlockSpec((1, tm, d), lambda bi, i: (bi, jnp.maximum(i - nc, 0), 0)),
            pl.BlockSpec((1, tm, d), lambda bi, i: (bi, jnp.minimum(i, nc - 1), 0)),
            pl.BlockSpec((1, 2, d), lambda bi, i: (jnp.where(i < nc, b, bi), 0, 0)),
            pl.BlockSpec((1, d), lambda bi, i: (0, 0)),
        ],
        out_specs=pl.BlockSpec((1, tm, d), lambda bi, i: (bi, i, 0)),
        compiler_params=_cparams(("arbitrary", "arbitrary")),
        name="norm_mod",
    )(x, ctx, ss, g)


def _matmul_kernel(a_ref, b_ref, o_ref):
    o_ref[...] = _dot(a_ref[...], b_ref[...]).astype(o_ref.dtype)


def _matmul(a, b, tm, tn, out_dtype, name):
    m, k = a.shape
    n = b.shape[1]
    return pl.pallas_call(
        _matmul_kernel,
        out_shape=jax.ShapeDtypeStruct((m, n), out_dtype),
        grid=(n // tn, m // tm),
        in_specs=[
            pl.BlockSpec((tm, k), lambda j, i: (i, 0)),
            pl.BlockSpec((k, tn), lambda j, i: (0, j)),
        ],
        out_specs=pl.BlockSpec((tm, tn), lambda j, i: (i, j)),
        compiler_params=_cparams(("arbitrary", "arbitrary")),
        name=name,
    )(a, b)


def _rope_tables(t_lat, t_ctx):
    rows = t_lat // GRID_W
    row = np.repeat(np.arange(rows), GRID_W).astype(np.float64)
    col = np.tile(np.arange(GRID_W), rows).astype(np.float64)
    half = DA_DH // 2
    inv = ROPE_BASE ** (-np.arange(0, half, 2, dtype=np.float64) / half)
    ar = row[:, None] * inv
    ac = col[:, None] * inv
    cr, sr, cc, sc = np.cos(ar), np.sin(ar), np.cos(ac), np.sin(ac)
    z = np.zeros_like(sr)
    cos64 = np.concatenate([cr, cr, cc, cc], axis=1)
    sa64 = np.concatenate([-sr, z, -sc, z], axis=1)
    sb64 = np.concatenate([z, sr, z, sc], axis=1)
    tile2 = lambda a: np.concatenate([a, a], axis=1)
    pad = lambda a, v: jnp.asarray(
        np.concatenate([np.full((t_ctx, LANES), v), tile2(a)], axis=0), F32)
    return pad(cos64, 1.0), pad(sa64, 0.0), pad(sb64, 0.0)


def _qk_prep_kernel(x_ref, cos_ref, sa_ref, sb_ref, g_ref, bd_ref, o_ref, *, is_q):
    lane = lax.broadcasted_iota(jnp.int32, cos_ref.shape, 1)
    for h in range(DA_HEADS):
        x = x_ref[0, :, h * LANES:(h + 1) * LANES].astype(F32)
        ss = _exact_lhs_dot_rhs(x * x, bd_ref[...])
        y = x * lax.rsqrt(ss * (1.0 / DA_DH) + EPS) * g_ref[...]
        up = pltpu.roll(y, LANES - 16, axis=1)
        dn = pltpu.roll(y, 16, axis=1)
        o = y * cos_ref[...] + up * sa_ref[...] + dn * sb_ref[...]
        if is_q:
            o = o * (DA_DH ** -0.5 * LOG2_E)
            zero = jnp.zeros_like(o)
            o_ref[0, :, 2 * h * LANES:(2 * h + 1) * LANES] = jnp.where(lane < DA_DH, o, zero).astype(o_ref.dtype)
            o_ref[0, :, (2 * h + 1) * LANES:(2 * h + 2) * LANES] = jnp.where(lane >= DA_DH, o, zero).astype(o_ref.dtype)
        else:
            o_ref[0, :, h * LANES:(h + 1) * LANES] = o.astype(o_ref.dtype)


def _exact_lhs_dot_rhs(x, m_bf16):
    p1, p2, p3 = _split3(x)
    return _dot(p1, m_bf16) + _dot(p2, m_bf16) + _dot(p3, m_bf16)


def _qk_prep(proj3, cos_t, sa_t, sb_t, gain, col0, is_q, t_ctx):
    b, r, _ = proj3.shape
    tm = ROW_TILE
    row0 = t_ctx // tm if is_q else 0
    n_rows = r - t_ctx if is_q else r
    width = DA_HEADS * (2 * LANES if is_q else LANES)
    bd = jnp.asarray(np.kron(np.eye(2), np.ones((DA_DH, DA_DH))), BF16)
    g2 = jnp.concatenate([gain, gain]).reshape(1, LANES).astype(F32)
    return pl.pallas_call(
        functools.partial(_qk_prep_kernel, is_q=is_q),
        out_shape=jax.ShapeDtypeStruct((b, n_rows, width), BF16),
        grid=(b, n_rows // tm),
        in_specs=[
            pl.BlockSpec((1, tm, DA_HEADS * LANES), lambda bi, i: (bi, row0 + i, col0 // DA_HEADS)),
            pl.BlockSpec((tm, LANES), lambda bi, i: (row0 + i, 0)),
            pl.BlockSpec((tm, LANES), lambda bi, i: (row0 + i, 0)),
            pl.BlockSpec((tm, LANES), lambda bi, i: (row0 + i, 0)),
            pl.BlockSpec((1, LANES), lambda bi, i: (0, 0)),
            pl.BlockSpec((LANES, LANES), lambda bi, i: (0, 0)),
        ],
        out_specs=pl.BlockSpec((1, tm, width), lambda bi, i: (bi, i, 0)),
        compiler_params=_cparams(("arbitrary", "arbitrary")),
        name="q_prep" if is_q else "k_prep",
    )(proj3, cos_t, sa_t, sb_t, g2, bd)


def _attn_body(q_ref, k_ref, v_ref, lam_ref, g_ref, o_ref, sp_ref, mp_ref, sc_ref, mc_ref, *, tk, nk):
    tq = q_ref.shape[1]
    q = q_ref[0]
    qs = jnp.concatenate([q[:, :LANES], q[:, LANES:]], axis=0)
    m_c = mc_ref[:, 0:1]
    ones = jnp.ones((tk, LANES), BF16)
    acc = jnp.zeros((2 * tq, 2 * LANES), F32)
    mpart = None
    for c in range(nk):
        sl = slice(c * tk, (c + 1) * tk)
        s = _dot_nt(qs, k_ref[0, sl, :])
        sp_ref[:, sl] = s
        for j in range(tk // LANES):
            blk = s[:, j * LANES:(j + 1) * LANES]
            mpart = blk if mpart is None else jnp.maximum(mpart, blk)
        p = jnp.exp2((sc_ref[:, sl] - m_c).astype(BF16))
        vaug = jnp.concatenate([v_ref[0, sl, :], ones], axis=1)
        acc = acc + _dot(p, vaug)
    mp_ref[...] = jnp.broadcast_to(jnp.max(mpart, axis=1, keepdims=True), mp_ref.shape)
    ratio = acc[:, :LANES] / acc[:, LANES:]
    lam = (jnp.exp(jnp.sum(lam_ref[0:1, :] * lam_ref[1:2, :], axis=1, keepdims=True))
           - jnp.exp(jnp.sum(lam_ref[2:3, :] * lam_ref[3:4, :], axis=1, keepdims=True)) + LAM_INIT)
    o = ratio[:tq] - lam * ratio[tq:]
    var = jnp.mean(o * o, axis=-1, keepdims=True)
    o = o * lax.rsqrt(var + EPS) * g_ref[...] * (1.0 - LAM_INIT)
    o_ref[0] = o.astype(o_ref.dtype)


def _attn_kernel(q_ref, k_ref, v_ref, lam_ref, g_ref, o_ref, sa_ref, ma_ref, sb_ref, mb_ref, *, tk, nk):
    f = pl.program_id(0)
    body = functools.partial(_attn_body, q_ref, k_ref, v_ref, lam_ref, g_ref, o_ref, tk=tk, nk=nk)

    @pl.when(f == 0)
    def _():
        sb_ref[...] = jnp.zeros(sb_ref.shape, F32)
        mb_ref[...] = jnp.zeros(mb_ref.shape, F32)

    @pl.when(lax.rem(f, 2) == 0)
    def _():
        body(sa_ref, ma_ref, sb_ref, mb_ref)

    @pl.when(lax.rem(f, 2) == 1)
    def _():
        body(sb_ref, mb_ref, sa_ref, ma_ref)


def _diff_attention(qm, kh, proj3, v_col0, lam_rows, subln_g):
    b, t, _ = qm.shape
    r = kh.shape[1]
    tq = ROW_TILE
    tk = ROW_TILE
    n_q = t // tq
    total = b * DA_HEADS * n_q

    def split(idx):
        return idx // (DA_HEADS * n_q), lax.rem(idx // n_q, DA_HEADS), lax.rem(idx, n_q)

    prod = lambda f: split(jnp.minimum(f, total - 1))
    cons = lambda f: split(jnp.maximum(f - 1, 0))
    return pl.pallas_call(
        functools.partial(_attn_kernel, tk=tk, nk=r // tk),
        out_shape=jax.ShapeDtypeStruct((b, t, DA_HEADS * DA_DV), BF16),
        grid=(total + 1,),
        in_specs=[
            pl.BlockSpec((1, tq, 2 * LANES), lambda f: (prod(f)[0], prod(f)[2], prod(f)[1])),
            pl.BlockSpec((1, r, LANES), lambda f: (prod(f)[0], 0, prod(f)[1])),
            pl.BlockSpec((1, r, LANES), lambda f: (cons(f)[0], 0, v_col0 + cons(f)[1])),
            pl.BlockSpec((8, LANES), lambda f: (0, 0)),
            pl.BlockSpec((1, LANES), lambda f: (0, 0)),
        ],
        out_specs=pl.BlockSpec((1, tq, DA_DV), lambda f: (cons(f)[0], cons(f)[2], cons(f)[1])),
        scratch_shapes=[pltpu.VMEM((2 * tq, r), F32), pltpu.VMEM((2 * tq, LANES), F32),
                        pltpu.VMEM((2 * tq, r), F32), pltpu.VMEM((2 * tq, LANES), F32)],
        compiler_params=_cparams(("arbitrary",)),
        name="diff_attention",
    )(qm, kh, proj3, lam_rows, subln_g.reshape(1, DA_DV).astype(F32))


def _log_sigmoid(z):
    return jnp.minimum(z, 0.0) - jnp.log(1.0 + jnp.exp(-jnp.abs(z)))


def _gla_kernel(*refs, reverse, final, n_chunks):
    if final:
        (q_ref, k_ref, v_ref, lr_ref, w2_ref, b_ref, tri_ref, ones_ref,
         of_ref, gr_ref, ng_ref, o_ref, st_ref) = refs
    else:
        (q_ref, k_ref, v_ref, lr_ref, w2_ref, b_ref, tri_ref, ones_ref, o_ref, st_ref) = refs
    c_len = GLA_CHUNK

    @pl.when(pl.program_id(2) == 0)
    def _():
        st_ref[...] = jnp.zeros(st_ref.shape, F32)

    q = q_ref[0].astype(F32) * (GLA_DK ** -0.5)
    k = k_ref[0].astype(F32)
    v = v_ref[0]
    z = _dot3(lr_ref[0], w2_ref[0]) + b_ref[0]
    la = _log_sigmoid(z) * (1.0 / GLA_TAU)
    tri = tri_ref[...]
    cum = _exact_lhs_dot(tri, la)
    tot = _exact_lhs_dot(ones_ref[...], la)
    q_in = (q * jnp.exp(cum)).astype(BF16)
    k_in = (k * jnp.exp(-cum)).astype(BF16)
    k_st = (k * jnp.exp(tot - cum)).astype(BF16)
    dec = jnp.exp(tot)
    att = jnp.where(tri > 0, _dot_nt(q_in, k_in), 0.0)
    o_intra = _dot(att.astype(BF16), v)

    order = range(n_chunks - 1, -1, -1) if reverse else range(n_chunks)
    outs = [None] * n_chunks
    for c in order:
        sl = slice(c * c_len, (c + 1) * c_len)
        st = st_ref[...]
        outs[c] = o_intra[sl] + _dot_nt(q_in[sl], st.astype(BF16))
        kv = _dot_tn(v[sl], k_st[sl])
        st_ref[...] = st * dec[c * c_len:c * c_len + 1, :] + kv
    o = jnp.concatenate(outs, axis=0)
    if final:
        o = o + of_ref[0].astype(F32)
        var = jnp.mean(o * o, axis=-1, keepdims=True)
        y = o * lax.rsqrt(var + EPS) * ng_ref[...]
        gr = gr_ref[0].astype(F32)
        y = y * (gr * (1.0 / (1.0 + jnp.exp(-gr))))
        o_ref[0] = y.astype(o_ref.dtype)
    else:
        o_ref[0] = o.astype(o_ref.dtype)


def _gla_direction(proj3, lr3, w2p, bias, t_ctx, cols, reverse, o_fwd=None, norm_g=None):
    b, r, _ = proj3.shape
    tc = ROW_TILE
    n_chunks = tc // GLA_CHUNK
    ng = r // tc
    ncg = t_ctx // tc
    final = o_fwd is not None
    q_col, k_col, v_col, gr_col = cols

    blk = np.kron(np.eye(n_chunks), np.ones((GLA_CHUNK, GLA_CHUNK)))
    full_tri = np.tril(np.ones((tc, tc)))
    tri = blk * (full_tri.T if reverse else full_tri)
    tri = jnp.asarray(tri, BF16)
    ones_bd = jnp.asarray(blk, BF16)

    if reverse:
        def grp(s):
            return jnp.where(s < ncg, ncg - 1 - s, ng - 1 - (s - ncg))
    else:
        def grp(s):
            return s

    def lat(s):
        return jnp.maximum(grp(jnp.maximum(s, ncg)) - ncg, 0)

    in_specs = [
        pl.BlockSpec((1, tc, GLA_DK), lambda bi, h, s: (bi, grp(s), q_col + h)),
        pl.BlockSpec((1, tc, GLA_DK), lambda bi, h, s: (bi, grp(s), k_col + h)),
        pl.BlockSpec((1, tc, GLA_DV), lambda bi, h, s: (bi, grp(s), v_col + h)),
        pl.BlockSpec((1, tc, LANES), lambda bi, h, s: (bi, grp(s), 0)),
        pl.BlockSpec((1, LANES, GLA_DK), lambda bi, h, s: (h, 0, 0)),
        pl.BlockSpec((1, 1, GLA_DK), lambda bi, h, s: (h, 0, 0)),
        pl.BlockSpec((tc, tc), lambda bi, h, s: (0, 0)),
        pl.BlockSpec((tc, tc), lambda bi, h, s: (0, 0)),
    ]
    args = [proj3, proj3, proj3, lr3, w2p, bias, tri, ones_bd]
    if final:
        in_specs += [
            pl.BlockSpec((1, tc, GLA_DV), lambda bi, h, s: (bi, lat(s), h)),
            pl.BlockSpec((1, tc, GLA_DV), lambda bi, h, s: (bi, grp(s), gr_col + h)),
            pl.BlockSpec((1, GLA_DV), lambda bi, h, s: (0, 0)),
        ]
        args += [o_fwd, proj3, norm_g.reshape(1, GLA_DV).astype(F32)]
    return pl.pallas_call(
        functools.partial(_gla_kernel, reverse=reverse, final=final, n_chunks=n_chunks),
        out_shape=jax.ShapeDtypeStruct((b, r - t_ctx, GLA_HEADS * GLA_DV), BF16 if final else F32),
        grid=(b, GLA_HEADS, ng),
        in_specs=in_specs,
        out_specs=pl.BlockSpec((1, tc, GLA_DV), lambda bi, h, s: (bi, lat(s), h)),
        scratch_shapes=[pltpu.VMEM((GLA_DV, GLA_DK), F32)],
        compiler_params=_cparams(("arbitrary", "arbitrary", "arbitrary")),
        name="gla_bwd" if reverse else "gla_fwd",
    )(*args)


def _sigmoid(x):
    return 1.0 / (1.0 + jnp.exp(-x))


def _merge_kernel(ya_ref, yb_ref, wa_ref, wb_ref, ga_ref, gb_ref, o_ref):
    a = _dot(ya_ref[0], wa_ref[...])
    bb = _dot(yb_ref[0], wb_ref[...])
    z = _sigmoid(ga_ref[0].astype(F32)) * a + _sigmoid(gb_ref[0].astype(F32)) * bb
    o_ref[0] = z.astype(o_ref.dtype)


def _merge(y_a, y_b, wa, wb, proj3, ga_col0, gb_col0, t_ctx):
    b, t, ka = y_a.shape
    kb = y_b.shape[2]
    d = wa.shape[1]
    tm, tn = ROW_TILE, 2048
    row0 = t_ctx // tm
    return pl.pallas_call(
        _merge_kernel,
        out_shape=jax.ShapeDtypeStruct((b, t, d), BF16),
        grid=(d // tn, b, t // tm),
        in_specs=[
            pl.BlockSpec((1, tm, ka), lambda j, bi, i: (bi, i, 0)),
            pl.BlockSpec((1, tm, kb), lambda j, bi, i: (bi, i, 0)),
            pl.BlockSpec((ka, tn), lambda j, bi, i: (0, j)),
            pl.BlockSpec((kb, tn), lambda j, bi, i: (0, j)),
            pl.BlockSpec((1, tm, tn), lambda j, bi, i: (bi, row0 + i, ga_col0 // tn + j)),
            pl.BlockSpec((1, tm, tn), lambda j, bi, i: (bi, row0 + i, gb_col0 // tn + j)),
        ],
        out_specs=pl.BlockSpec((1, tm, tn), lambda j, bi, i: (bi, i, j)),
        compiler_params=_cparams(("arbitrary", "arbitrary", "arbitrary")),
        name="merge",
    )(y_a, y_b, wa, wb, proj3, proj3)


def _out_proj_kernel(z_ref, w_ref, x_ref, g_ref, o_ref):
    y = _dot(z_ref[0], w_ref[...])
    o_ref[0] = x_ref[0] + g_ref[0] * y


def _out_proj_residual(z, w, x, gate):
    b, t, d = x.shape
    tm, tn = 512, 2048
    return pl.pallas_call(
        _out_proj_kernel,
        out_shape=jax.ShapeDtypeStruct((b, t, d), F32),
        grid=(d // tn, b, t // tm),
        in_specs=[
            pl.BlockSpec((1, tm, d), lambda j, bi, i: (bi, i, 0)),
            pl.BlockSpec((d, tn), lambda j, bi, i: (0, j)),
            pl.BlockSpec((1, tm, tn), lambda j, bi, i: (bi, i, j)),
            pl.BlockSpec((1, 1, tn), lambda j, bi, i: (bi, 0, j)),
        ],
        out_specs=pl.BlockSpec((1, tm, tn), lambda j, bi, i: (bi, i, j)),
        compiler_params=_cparams(("arbitrary", "arbitrary", "arbitrary")),
        name="out_proj",
    )(z, w, x, gate)


def _peer_q_kernel(x_ref, ss_ref, g_ref, w_ref, h_ref, q_ref):
    h = _rms_mod(x_ref[0], g_ref[...], ss_ref[0, 0:1, :], ss_ref[0, 1:2, :]).astype(BF16)
    h_ref[0] = h
    q_ref[0] = _dot(h, w_ref[...])


def _peer_query(x1, ss, g, wq):
    b, t, d = x1.shape
    n = wq.shape[1]
    tm = ROW_TILE
    return pl.pallas_call(
        _peer_q_kernel,
        out_shape=(jax.ShapeDtypeStruct((b, t, d), BF16), jax.ShapeDtypeStruct((b, t, n), F32)),
        grid=(b, t // tm),
        in_specs=[
            pl.BlockSpec((1, tm, d), lambda bi, i: (bi, i, 0)),
            pl.BlockSpec((1, 2, d), lambda bi, i: (bi, 0, 0)),
            pl.BlockSpec((1, d), lambda bi, i: (0, 0)),
            pl.BlockSpec((d, n), lambda bi, i: (0, 0)),
        ],
        out_specs=(pl.BlockSpec((1, tm, d), lambda bi, i: (bi, i, 0)),
                   pl.BlockSpec((1, tm, n), lambda bi, i: (bi, i, 0))),
        compiler_params=_cparams(("arbitrary", "arbitrary")),
        name="peer_query",
    )(x1, ss, g, wq)


SUBLANES = 8


def _cmp_exchange(xs, hi, lo):
    a, b = xs[hi], xs[lo]
    if a is None and b is None:
        return
    if a is None or b is None:
        xs[hi], xs[lo] = (b if a is None else a), None
        return
    xs[hi], xs[lo] = jnp.maximum(a, b), jnp.minimum(a, b)


def _bitonic_sort_desc(xs):
    n = len(xs)
    xs = list(xs)
    k = 2
    while k <= n:
        j = k // 2
        while j >= 1:
            for i in range(n):
                l = i ^ j
                if l > i:
                    if (i & k) == 0:
                        _cmp_exchange(xs, i, l)
                    else:
                        _cmp_exchange(xs, l, i)
            j //= 2
        k *= 2
    return xs


def _bitonic_merge_desc(xs):
    n = len(xs)
    xs = list(xs)
    j = n // 2
    while j >= 1:
        for i in range(n):
            l = i ^ j
            if l > i:
                _cmp_exchange(xs, i, l)
        j //= 2
    return xs


def _top16_sorted(xs):
    n = PEER_TOPK
    xs = list(xs) + [None] * (n - len(xs))
    xs = _bitonic_sort_desc(xs)
    shift = SUBLANES // 2
    while shift >= 1:
        merged = []
        for i in range(n):
            a, b = xs[i], xs[n - 1 - i]
            if b is not None:
                b = pltpu.roll(b, shift, axis=0)
            merged.append(b if a is None else (a if b is None else jnp.maximum(a, b)))
        xs = _bitonic_merge_desc(merged)
        shift //= 2
    return xs


def _route_kernel(q_ref, keys_ref, s1_ref, s2_ref, tau_ref):
    q = q_ref[0]
    nk = PEER_NKEYS
    s1 = _dot3_nt(keys_ref[0, 0], q[:, :nk]) * LOG2_E
    s2 = _dot3_nt(keys_ref[0, 1], q[:, nk:]) * LOG2_E
    split = lambda s: [s[i * SUBLANES:(i + 1) * SUBLANES] for i in range(nk // SUBLANES)]
    v1 = _top16_sorted(split(s1))
    v2 = _top16_sorted(split(s2))
    sub = lax.broadcasted_iota(jnp.int32, v1[0].shape, 0)

    def spread(vals):
        out = vals[0]
        for j in range(1, SUBLANES):
            out = jnp.where(sub == j, vals[j], out)
        return out

    v2_lo, v2_hi = spread(v2[:SUBLANES]), spread(v2[SUBLANES:])

    def top_pairs(w1):
        cands = [w1[0] + v2_lo, w1[0] + v2_hi]
        cands += [w1[i] + v2_lo for i in range(1, SUBLANES)]
        cands += [spread(w1[SUBLANES:]) + v2[0]]
        return _top16_sorted(cands)

    top = top_pairs(v1)
    m = top[0]
    zsum = jnp.zeros_like(m)
    for c in top:
        zsum = zsum + jnp.exp2(c - m)
    off = m + jnp.log2(zsum)
    tau = top_pairs([v - off for v in v1])[PEER_TOPK - 1]
    s1_ref[0, 0] = s1 - off[0:1]
    s2_ref[0, 0] = s2
    tau_ref[0, 0] = tau[0:1]


def _peer_route(q, keys):
    b, t, _ = q.shape
    tt = 512
    nk = PEER_NKEYS
    big = jax.ShapeDtypeStruct((b, PEER_HEADS, nk, t), F32)
    spec = pl.BlockSpec((1, 1, nk, tt), lambda bi, i, h: (bi, h, 0, i))
    return pl.pallas_call(
        _route_kernel,
        out_shape=(big, big, jax.ShapeDtypeStruct((b, PEER_HEADS, 1, t), F32)),
        grid=(b, t // tt, PEER_HEADS),
        in_specs=[
            pl.BlockSpec((1, tt, 2 * nk), lambda bi, i, h: (bi, i, h)),
            pl.BlockSpec((1, 2, nk, nk), lambda bi, i, h: (h, 0, 0, 0)),
        ],
        out_specs=(spec, spec,
                   pl.BlockSpec((1, 1, 1, tt), lambda bi, i, h: (bi, h, 0, i))),
        compiler_params=_cparams(("arbitrary", "arbitrary", "arbitrary")),
        name="peer_route",
    )(q, keys)


def _gelu(x):
    return 0.5 * x * (1.0 + lax.erf(x * (2.0 ** -0.5)))


def _peer_mix_kernel(h_ref, u_ref, vt_ref, s1_ref, s2_ref, tau_ref,
                     o_ref, st_ref, gt_ref, *, n_a, n_e):
    f = pl.program_id(0)
    nk = PEER_NKEYS
    slot_p = lax.rem(f, 2)
    slot_c = 1 - slot_p
    e_c = lax.rem(jnp.maximum(f - 1, 0), n_e)

    @pl.when(f == 0)
    def _():
        st_ref[1] = jnp.zeros(st_ref.shape[1:], F32)
        gt_ref[1] = jnp.zeros(gt_ref.shape[1:], F32)

    @pl.when(e_c == 0)
    def _():
        o_ref[...] = jnp.zeros(o_ref.shape, F32)

    a_t = (_gelu(st_ref[slot_c]) * gt_ref[slot_c]).astype(BF16)
    o_ref[0] += _dot(vt_ref[...], a_t)

    st_ref[slot_p] = _dot_nt(u_ref[...], h_ref[0])
    for al in range(n_a):
        gate = None
        for h in range(PEER_HEADS):
            cand = s1_ref[0, h, al:al + 1, :] + s2_ref[0, h]
            term = jnp.where(cand >= tau_ref[0, h], jnp.exp2(cand), 0.0)
            gate = term if gate is None else gate + term
        gt_ref[slot_p, al * nk:(al + 1) * nk, :] = gate


def _peer_mix(h2, u_bf, vt_bf, s1, s2, tau):
    b, t, d = h2.shape
    nk = PEER_NKEYS
    n_exp = u_bf.shape[0]
    tt = 512
    n_a = 8
    eb = n_a * nk
    n_e = n_exp // eb
    n_t = t // tt
    total = b * n_t * n_e

    def split(idx):
        return idx // (n_t * n_e), lax.rem(idx // n_e, n_t), lax.rem(idx, n_e)

    prod = lambda f: split(jnp.minimum(f, total - 1))
    cons = lambda f: split(jnp.maximum(f - 1, 0))

    def tok_spec(rows):
        def imap(f):
            bi, ti, _ = prod(f)
            return (bi, 0, 0, ti)
        return pl.BlockSpec((1, PEER_HEADS, rows, tt), imap)

    def blk_spec():
        def imap(f):
            bi, ti, e = prod(f)
            return (bi, 0, e, ti)
        return pl.BlockSpec((1, PEER_HEADS, n_a, tt), imap)

    return pl.pallas_call(
        functools.partial(_peer_mix_kernel, n_a=n_a, n_e=n_e),
        out_shape=jax.ShapeDtypeStruct((b, d, t), F32),
        grid=(total + 1,),
        in_specs=[
            pl.BlockSpec((1, tt, d), lambda f: (prod(f)[0], prod(f)[1], 0)),
            pl.BlockSpec((eb, d), lambda f: (prod(f)[2], 0)),
            pl.BlockSpec((d, eb), lambda f: (0, cons(f)[2])),
            blk_spec(), tok_spec(nk), tok_spec(1),
        ],
        out_specs=pl.BlockSpec((1, d, tt), lambda f: (cons(f)[0], 0, cons(f)[1])),
        scratch_shapes=[pltpu.VMEM((2, eb, tt), F32), pltpu.VMEM((2, eb, tt), F32)],
        compiler_params=_cparams(("arbitrary",)),
        name="peer_mix",
    )(h2, u_bf, vt_bf, s1, s2, tau)


def _peer_out_kernel(acc_ref, x_ref, g_ref, o_ref):
    o_ref[0] = x_ref[0] + g_ref[0] * acc_ref[0].T


def _peer_residual(acc_t, x1, gate):
    b, t, d = x1.shape
    tt = 512
    return pl.pallas_call(
        _peer_out_kernel,
        out_shape=jax.ShapeDtypeStruct((b, t, d), F32),
        grid=(b, t // tt),
        in_specs=[
            pl.BlockSpec((1, d, tt), lambda bi, i: (bi, 0, i)),
            pl.BlockSpec((1, tt, d), lambda bi, i: (bi, i, 0)),
            pl.BlockSpec((1, 1, d), lambda bi, i: (bi, 0, 0)),
        ],
        out_specs=pl.BlockSpec((1, tt, d), lambda bi, i: (bi, i, 0)),
        compiler_params=_cparams(("arbitrary", "arbitrary")),
        name="peer_residual",
    )(acc_t, x1, gate)


def kernel(x, c, ctx, c_ctx, ada_w, ada_b, norm1_g, norm2_g, w_in, da_qn_g, da_kn_g, da_lam_q1, da_lam_k1, da_lam_q2, da_lam_k2, da_subln_g, gla_w2_f, gla_b_f, gla_w2_b, gla_b_b, gla_norm_g, w_br_a, w_br_b, w_out, peer_wq, peer_keys, peer_u, peer_v):
    b, t, d = x.shape
    t_ctx = ctx.shape[1]
    assert ada_w.shape[0] == 1, "single-layer kernel"
    assert t % ROW_TILE == 0 and t_ctx % ROW_TILE == 0 and t % GRID_W == 0
    l = 0

    cond = jnp.concatenate([c, c_ctx[None, :]], axis=0)
    mod = _ada_modulation(cond, ada_w[l], ada_b[l])[:b + 1].reshape(b + 1, 6, d)
    ss1 = mod[:, 0:2]
    g1 = mod[:b, 2:3]
    ss2 = mod[:b, 3:5]
    g2 = mod[:b, 5:6]

    h_all = _norm_mod_all(x, ctx, ss1, norm1_g[l].reshape(1, d))
    r = t_ctx + t
    wi = w_in[l]
    o_glr = 6144
    w_main = jnp.concatenate([wi[:, :o_glr], wi[:, o_glr + 2 * GLA_RANK:]], axis=1).astype(BF16)
    w_lr = jnp.pad(wi[:, o_glr:o_glr + 2 * GLA_RANK], ((0, 0), (0, LANES - 2 * GLA_RANK))).astype(BF16)
    h2d = h_all.reshape(b * r, d)
    n_main = w_main.shape[1]
    proj3 = _matmul(h2d, w_main, 512, 2048, BF16, "in_proj").reshape(b, r, n_main)
    lr3 = _matmul(h2d, w_lr, 512, LANES, F32, "in_proj_lr").reshape(b, r, LANES)

    cos_t, sa_t, sb_t = _rope_tables(t, t_ctx)
    qm = _qk_prep(proj3, cos_t, sa_t, sb_t, da_qn_g[l], 0, True, t_ctx)
    kh = _qk_prep(proj3, cos_t, sa_t, sb_t, da_kn_g[l], 8, False, t_ctx)
    lam_rows = jnp.zeros((8, LANES), F32)
    lam_rows = lam_rows.at[0, :DA_DH].set(da_lam_q1[l]).at[1, :DA_DH].set(da_lam_k1[l])
    lam_rows = lam_rows.at[2, :DA_DH].set(da_lam_q2[l]).at[3, :DA_DH].set(da_lam_k2[l])
    y_a = _diff_attention(qm, kh, proj3, 16, lam_rows, da_subln_g[l])

    def pad_w2(w2, row0):
        w = w2.reshape(GLA_RANK, GLA_HEADS, GLA_DK).transpose(1, 0, 2)
        return jnp.pad(w, ((0, 0), (row0, LANES - GLA_RANK - row0), (0, 0))).astype(F32)

    cols = (24, 28, 16, 20)
    o_f = _gla_direction(proj3, lr3, pad_w2(gla_w2_f[l], 0),
                         gla_b_f[l].reshape(GLA_HEADS, 1, GLA_DK), t_ctx, cols, False)
    y_b = _gla_direction(proj3, lr3, pad_w2(gla_w2_b[l], GLA_RANK),
                         gla_b_b[l].reshape(GLA_HEADS, 1, GLA_DK), t_ctx, cols, True,
                         o_fwd=o_f, norm_g=gla_norm_g[l])

    z = _merge(y_a, y_b, w_br_a[l].astype(BF16), w_br_b[l].astype(BF16), proj3, 6144, 8192, t_ctx)
    x1 = _out_proj_residual(z, w_out[l].astype(BF16), x, g1)

    h2, q = _peer_query(x1, ss2, norm2_g[l].reshape(1, d), peer_wq[l].astype(BF16))
    s1, s2, tau = _peer_route(q, peer_keys[l])
    acc_t = _peer_mix(h2, peer_u[l].astype(BF16), peer_v[l].T.astype(BF16), s1, s2, tau)
    return _peer_residual(acc_t, x1, g2)
```
